```python
import math
import jax, jax.numpy as jnp
from jax import lax
import numpy as np

D_MODEL = 2048
BATCH = 8
SEQ = 2048
DEPTH = 2
DEC_BATCH = 16
DEC_SEQ = 32
PAST_LEN = 2048

CHUNK = 64
Q_BLOCK = 128
H_A = 8
D_A = 64
DV_A = 2 * D_A
H_B = 8
D_B = 128
QK_A = H_A * 2 * D_A
V_A = H_A * DV_A
W_B = H_B * D_B
IN_COLS = 2 * QK_A + V_A + 3 * W_B + 2 * D_MODEL
NUM_BUCKETS = 32
MAX_DISTANCE = 128
D_FF = 5632
N_EXPERTS = 8
TOP_K = 2
D_FF_E = 7168
N_DENSE = (DEPTH + 1) // 2
N_MOE = DEPTH // 2
EPS = 1e-6
NEG_INF = -1e30

kernel_name = "chunk_streaming_diffattn_stickbreak_gated_hybrid_step"


def _rmsnorm(x, g):
    xf = x.astype(jnp.float32)
    y = xf * lax.rsqrt(jnp.mean(xf * xf, axis=-1, keepdims=True) + EPS)
    return (y * g.astype(jnp.float32)).astype(x.dtype)


def _lambda_init(layer):
    return 0.8 - 0.6 * math.exp(-0.3 * layer)


def _t5_bucket(rel):
    half = NUM_BUCKETS // 2
    max_exact = half // 2
    n = jnp.abs(rel)
    far = max_exact + (jnp.log(jnp.maximum(n, 1).astype(jnp.float32) / max_exact)
                       / math.log(MAX_DISTANCE / max_exact) * (half - max_exact)).astype(jnp.int32)
    far = jnp.minimum(far, half - 1)
    return jnp.where(rel > 0, half, 0) + jnp.where(n < max_exact, n, far)


def _sweep(block_fn, q, q_pos):
    b, t = q.shape[0], q.shape[1]
    if t <= Q_BLOCK:
        return block_fn(q, q_pos)
    nb = t // Q_BLOCK
    qb = jnp.moveaxis(q.reshape((b, nb, Q_BLOCK) + q.shape[2:]), 1, 0)
    pb = q_pos.reshape(nb, Q_BLOCK)
    out = lax.map(lambda a: block_fn(a[0], a[1]), (qb, pb))
    out = jnp.moveaxis(out, 0, 1)
    return out.reshape((b, t) + out.shape[3:])


def _diff_attention(q, k, v, q_pos, k_pos, lam, rel_bias):
    scale = D_A ** -0.5

    def block(qb, pb):
        s = jnp.einsum('bthcd,bshcd->bhcts', qb, k).astype(jnp.float32) * scale
        rel = k_pos[None, :] - pb[:, None]
        bias = jnp.transpose(rel_bias[_t5_bucket(rel)].astype(jnp.float32), (2, 0, 1))
        mask = (k_pos[None, :] // CHUNK) <= (pb[:, None] // CHUNK)
        s = jnp.where(mask, s + bias[None, :, None], NEG_INF)
        p = jax.nn.softmax(s, axis=-1)
        w = p[:, :, 0] - lam * p[:, :, 1]
        return jnp.einsum('bhts,bshe->bthe', w.astype(v.dtype), v)

    return _sweep(block, q, q_pos)


def _stick_breaking(q, k, v, q_pos, k_pos):
    scale = D_B ** -0.5

    def block(qb, pb):
        z = jnp.einsum('bthd,bshd->bhts', qb, k).astype(jnp.float32) * scale
        causal = k_pos[None, :] < pb[:, None]
        log_1m = jnp.where(causal, jax.nn.log_sigmoid(-z), 0.0)
        between = lax.cumsum(log_1m, axis=3, reverse=True) - log_1m
        a = jnp.where(causal, jnp.exp(jax.nn.log_sigmoid(z) + between), 0.0)
        return jnp.einsum('bhts,bshd->bthd', a.astype(v.dtype), v)

    return _sweep(block, q, q_pos)


def _mixer(h, past, q_pos, k_pos, w_in, lq1, lk1, lq2, lk2, subln_g, w_branch_a, w_branch_b,
           w_out, rel_bias, layer):
    b, t, _ = h.shape
    proj = h @ w_in
    cuts = [QK_A, 2 * QK_A, 2 * QK_A + V_A, 2 * QK_A + V_A + W_B, 2 * QK_A + V_A + 2 * W_B,
            2 * QK_A + V_A + 3 * W_B, 2 * QK_A + V_A + 3 * W_B + D_MODEL]
    qa, ka, va, qb, kb, vb, ga, gb = jnp.split(proj, cuts, axis=-1)
    ka = ka.reshape(b, t, H_A, 2 * D_A)
    va = va.reshape(b, t, H_A, DV_A)
    kb = kb.reshape(b, t, H_B, D_B)
    vb = vb.reshape(b, t, H_B, D_B)
    if past is None:
        ka_all, va_all, kb_all, vb_all = ka, va, kb, vb
    else:
        ka_all = jnp.concatenate([past[0], ka], axis=1)
        va_all = jnp.concatenate([past[1], va], axis=1)
        kb_all = jnp.concatenate([past[2], kb], axis=1)
        vb_all = jnp.concatenate([past[3], vb], axis=1)
    s_len = ka_all.shape[1]
    lam_init = _lambda_init(layer)
    lam = (jnp.exp(jnp.sum(lq1.astype(jnp.float32) * lk1.astype(jnp.float32)))
           - jnp.exp(jnp.sum(lq2.astype(jnp.float32) * lk2.astype(jnp.float32))) + lam_init)
    oa = _diff_attention(qa.reshape(b, t, H_A, 2, D_A), ka_all.reshape(b, s_len, H_A, 2, D_A),
                         va_all, q_pos, k_pos, lam, rel_bias)
    oa = _rmsnorm(oa, subln_g) * (1.0 - lam_init)
    ob = _stick_breaking(qb.reshape(b, t, H_B, D_B), kb_all, vb_all, q_pos, k_pos)
    merged = (jax.nn.sigmoid(ga) * (oa.reshape(b, t, V_A) @ w_branch_a)
              + jax.nn.sigmoid(gb) * (ob.reshape(b, t, W_B) @ w_branch_b))
    return merged @ w_out, (ka, va, kb, vb)


def _swiglu(h, wg, wu, wd):
    return (jax.nn.silu(h @ wg) * (h @ wu)) @ wd


def _moe(h, w_r, b_r, wg, wu, wd):
    logits = (h @ w_r).astype(jnp.float32) + b_r.astype(jnp.float32)
    top_v, top_i = lax.top_k(logits, TOP_K)
    gates = jax.nn.softmax(top_v, axis=-1)
    combine = jnp.sum(jax.nn.one_hot(top_i, N_EXPERTS, dtype=jnp.float32) * gates[..., None],
                      axis=-2).astype(h.dtype)
    out = jnp.zeros_like(h)
    for e in range(N_EXPERTS):
        out = out + combine[..., e:e + 1] * _swiglu(h, wg[e], wu[e], wd[e])
    return out


def setup_inputs(seed: int = 0) -> dict:
    key = jax.random.key(seed)
    ks = jax.random.split(key, 32)

    def nrm(k, shape, scale):
        return jax.random.normal(k, shape, jnp.float32) * scale

    d = D_MODEL
    return {
        "x_prompt": nrm(ks[0], (BATCH, SEQ, d), 1.0),
        "x_sample": nrm(ks[1], (DEC_BATCH, DEC_SEQ, d), 1.0),
        "cache_diff_k": nrm(ks[2], (DEPTH, DEC_BATCH, PAST_LEN, H_A, 2 * D_A), 1.0),
        "cache_diff_v": nrm(ks[3], (DEPTH, DEC_BATCH, PAST_LEN, H_A, DV_A), 1.0),
        "cache_sb_k": nrm(ks[4], (DEPTH, DEC_BATCH, PAST_LEN, H_B, D_B), 1.0),
        "cache_sb_v": nrm(ks[5], (DEPTH, DEC_BATCH, PAST_LEN, H_B, D_B), 1.0),
        "rel_bias": nrm(ks[6], (NUM_BUCKETS, H_A), 0.2),
        "norm_mix_g": 1.0 + nrm(ks[7], (DEPTH, d), 0.01),
        "w_in": nrm(ks[8], (DEPTH, d, IN_COLS), d ** -0.5),
        "lambda_q1": nrm(ks[9], (DEPTH, D_A), 0.1),
        "lambda_k1": nrm(ks[10], (DEPTH, D_A), 0.1),
        "lambda_q2": nrm(ks[11], (DEPTH, D_A), 0.1),
        "lambda_k2": nrm(ks[12], (DEPTH, D_A), 0.1),
        "subln_g": 1.0 + nrm(ks[13], (DEPTH, DV_A), 0.01),
        "w_branch_a": nrm(ks[14], (DEPTH, V_A, d), V_A ** -0.5),
        "w_branch_b": nrm(ks[15], (DEPTH, W_B, d), W_B ** -0.5),
        "w_out": nrm(ks[16], (DEPTH, d, d), d ** -0.5),
        "norm_ffn_g": 1.0 + nrm(ks[17], (DEPTH, d), 0.01),
        "dense_w_gate": nrm(ks[18], (N_DENSE, d, D_FF), d ** -0.5),
        "dense_w_up": nrm(ks[19], (N_DENSE, d, D_FF), d ** -0.5),
        "dense_w_down": nrm(ks[20], (N_DENSE, D_FF, d), D_FF ** -0.5),
        "router_w": nrm(ks[21], (N_MOE, d, N_EXPERTS), d ** -0.5),
        "router_b": nrm(ks[22], (N_MOE, N_EXPERTS), 0.01),
        "moe_w_gate": nrm(ks[23], (N_MOE, N_EXPERTS, d, D_FF_E), d ** -0.5),
        "moe_w_up": nrm(ks[24], (N_MOE, N_EXPERTS, d, D_FF_E), d ** -0.5),
        "moe_w_down": nrm(ks[25], (N_MOE, N_EXPERTS, D_FF_E, d), D_FF_E ** -0.5),
        "final_norm_g": 1.0 + nrm(ks[26], (d,), 0.01),
    }


def reference(x_prompt, x_sample, cache_diff_k, cache_diff_v, cache_sb_k, cache_sb_v, rel_bias,
              norm_mix_g, w_in, lambda_q1, lambda_k1, lambda_q2, lambda_k2, subln_g, w_branch_a,
              w_branch_b, w_out, norm_ffn_g, dense_w_gate, dense_w_up, dense_w_down, router_w,
              router_b, moe_w_gate, moe_w_up, moe_w_down, final_norm_g):
    t_p = x_prompt.shape[1]
    t_s = x_sample.shape[1]
    past = cache_diff_k.shape[2]
    pos_p = jnp.arange(t_p, dtype=jnp.int32)
    pos_sq = past + jnp.arange(t_s, dtype=jnp.int32)
    pos_sk = jnp.arange(past + t_s, dtype=jnp.int32)
    xp, xs = x_prompt, x_sample
    rows_p = ([], [], [], [])
    rows_s = ([], [], [], [])
    for l in range(DEPTH):
        mix_w = (w_in[l], lambda_q1[l], lambda_k1[l], lambda_q2[l], lambda_k2[l], subln_g[l],
                 w_branch_a[l], w_branch_b[l], w_out[l], rel_bias, l)
        op, new_p = _mixer(_rmsnorm(xp, norm_mix_g[l]), None, pos_p, pos_p, *mix_w)
        os_, new_s = _mixer(_rmsnorm(xs, norm_mix_g[l]),
                            (cache_diff_k[l], cache_diff_v[l], cache_sb_k[l], cache_sb_v[l]),
                            pos_sq, pos_sk, *mix_w)
        xp = xp + op
        xs = xs + os_
        for i in range(4):
            rows_p[i].append(new_p[i])
            rows_s[i].append(new_s[i])
        hp = _rmsnorm(xp, norm_ffn_g[l])
        hs = _rmsnorm(xs, norm_ffn_g[l])
        if l % 2 == 0:
            j = l // 2
            xp = xp + _swiglu(hp, dense_w_gate[j], dense_w_up[j], dense_w_down[j])
            xs = xs + _swiglu(hs, dense_w_gate[j], dense_w_up[j], dense_w_down[j])
        else:
            j = l // 2
            xp = xp + _moe(hp, router_w[j], router_b[j], moe_w_gate[j], moe_w_up[j], moe_w_down[j])
            xs = xs + _moe(hs, router_w[j], router_b[j], moe_w_gate[j], moe_w_up[j], moe_w_down[j])
    y_prompt = _rmsnorm(xp, final_norm_g)
    y_sample = _rmsnorm(xs, final_norm_g)
    diff_k_prompt = jnp.stack(rows_p[0])
    diff_v_prompt = jnp.stack(rows_p[1])
    sb_k_prompt = jnp.stack(rows_p[2])
    sb_v_prompt = jnp.stack(rows_p[3])
    diff_k_sample = jnp.stack(rows_s[0])
    diff_v_sample = jnp.stack(rows_s[1])
    sb_k_sample = jnp.stack(rows_s[2])
    sb_v_sample = jnp.stack(rows_s[3])
    return (y_prompt, y_sample, diff_k_prompt, diff_v_prompt, sb_k_prompt, sb_v_prompt,
            diff_k_sample, diff_v_sample, sb_k_sample, sb_v_sample)
```

```python
import functools
import math

import jax
import jax.numpy as jnp
from jax import lax
from jax.experimental import pallas as pl
from jax.experimental.pallas import tpu as pltpu

CHUNK = 64
NUM_BUCKETS = 32
MAX_DISTANCE = 128
TOP_K = 2
EPS = 1e-6
NEG_INF = -1e30

LANES = 128
VMEM_LIMIT_BYTES = 56 * 1024 * 1024

BF16 = jnp.bfloat16
F32 = jnp.float32


def _params(*sem):
    return pltpu.CompilerParams(dimension_semantics=sem, vmem_limit_bytes=VMEM_LIMIT_BYTES)


def _row_tile(n, want):
    t = min(n, want)
    while n % t:
        t //= 2
    return t


def _rmsnorm_kernel(x_ref, g_ref, *o_refs):
    x = x_ref[...].astype(F32)
    y = x * lax.rsqrt(jnp.mean(x * x, axis=-1, keepdims=True) + EPS) * g_ref[...]
    for o_ref in o_refs:
        o_ref[...] = y.astype(o_ref.dtype)


def _rmsnorm(x, g, out_dtypes, name):
    n, d = x.shape
    tm = _row_tile(n, 512)
    outs = pl.pallas_call(
        _rmsnorm_kernel,
        grid=(n // tm,),
        in_specs=[pl.BlockSpec((tm, d), lambda i: (i, 0)),
                  pl.BlockSpec((1, d), lambda i: (0, 0))],
        out_specs=[pl.BlockSpec((tm, d), lambda i: (i, 0)) for _ in out_dtypes],
        out_shape=[jax.ShapeDtypeStruct((n, d), dt) for dt in out_dtypes],
        compiler_params=_params("parallel"),
        name=name,
    )(x, g.reshape(1, d).astype(F32))
    return outs


def _mm_kernel(x_ref, w_ref, *rest, has_res):
    if has_res:
        r_ref, o_refs = rest[0], rest[1:]
    else:
        r_ref, o_refs = None, rest
    acc = jnp.dot(x_ref[...], w_ref[...], preferred_element_type=F32)
    if has_res:
        acc = acc + r_ref[...]
    for o_ref in o_refs:
        o_ref[...] = acc.astype(o_ref.dtype)


def _matmul(x, w, col_block_fn, n_blocks, tn, out_dtypes, name, res=None):
    n, k = x.shape
    tm = _row_tile(n, 512)
    in_specs = [pl.BlockSpec((tm, k), lambda j, i: (i, 0)),
                pl.BlockSpec((k, tn), lambda j, i: (0, col_block_fn(j)))]
    args = [x, w]
    if res is not None:
        in_specs.append(pl.BlockSpec((tm, tn), lambda j, i: (i, j)))
        args.append(res)
    return pl.pallas_call(
        functools.partial(_mm_kernel, has_res=res is not None),
        grid=(n_blocks, n // tm),
        in_specs=in_specs,
        out_specs=[pl.BlockSpec((tm, tn), lambda j, i: (i, j)) for _ in out_dtypes],
        out_shape=[jax.ShapeDtypeStruct((n, n_blocks * tn), dt) for dt in out_dtypes],
        compiler_params=_params("parallel", "parallel"),
        name=name,
    )(*args)


def _t5_bucket(rel):
    half = NUM_BUCKETS // 2
    max_exact = half // 2
    n = jnp.abs(rel)
    far = max_exact + (jnp.log(jnp.maximum(n, 1).astype(F32) / max_exact)
                       / math.log(MAX_DISTANCE / max_exact) * (half - max_exact)).astype(jnp.int32)
    far = jnp.minimum(far, half - 1)
    return jnp.where(rel > 0, half, 0) + jnp.where(n < max_exact, n, far)


def _masked_bias(rel_bias, q_pos, k_pos):
    rel = k_pos[None, :] - q_pos[:, None]
    bias = jnp.transpose(rel_bias[_t5_bucket(rel)].astype(F32), (2, 0, 1))
    mask = (k_pos[None, :] // CHUNK) <= (q_pos[:, None] // CHUNK)
    return jnp.where(mask[None], bias, NEG_INF)


def _prompt_bias_tiles(rel_bias, t):
    assert t >= MAX_DISTANCE and t % CHUNK == 0
    pos = jnp.arange(t, dtype=jnp.int32)
    tiles = [_masked_bias(rel_bias, 2 * t + pos, (2 - d) * t + pos) for d in (2, 1, 0)]
    return jnp.stack(tiles, axis=1)


def _split_maps(q, d_a):
    lane = lax.broadcasted_iota(jnp.int32, q.shape, 1)
    zero = jnp.zeros_like(q)
    return jnp.concatenate([jnp.where(lane < d_a, q, zero), jnp.where(lane >= d_a, q, zero)], axis=0)


def _diff_finish(o, lq1, lk1, lq2, lk2, g, t, lam_init):
    lam = (jnp.exp(jnp.sum(lq1 * lk1, axis=-1, keepdims=True))
           - jnp.exp(jnp.sum(lq2 * lk2, axis=-1, keepdims=True)) + lam_init)
    oa = o[:t] - lam * o[t:]
    y = oa * lax.rsqrt(jnp.mean(oa * oa, axis=-1, keepdims=True) + EPS)
    return (y * g) * (1.0 - lam_init)


def _nt_dot(a, b):
    return lax.dot_general(a, b, (((1,), (1,)), ((), ())), preferred_element_type=F32)


def _diff_prompt_kernel(q_ref, k_ref, v_ref, bias_ref, lq1_ref, lk1_ref, lq2_ref, lk2_ref, g_ref,
                        o_ref, *, t, d_a, lam_init):
    qi = pl.program_id(2)
    qq = _split_maps(q_ref[...], d_a)
    scale = d_a ** -0.5
    dv = v_ref.shape[-1]

    def body(j, carry):
        m, l, acc = carry
        start = pl.multiple_of(j * t, t)
        kb = k_ref[pl.ds(start, t), :]
        vb = v_ref[pl.ds(start, t), :]
        bt = bias_ref[jnp.clip(j - qi + 2, 0, 2)]
        s = _nt_dot(qq, kb) * scale + jnp.concatenate([bt, bt], axis=0)
        m_new = jnp.maximum(m, jnp.max(s, axis=1, keepdims=True))
        alpha = jnp.exp(m - m_new)
        p = jnp.exp(s - m_new)
        l = alpha * l + jnp.sum(p, axis=1, keepdims=True)
        acc = alpha * acc + jnp.dot(p.astype(BF16), vb, preferred_element_type=F32)
        return m_new, l, acc

    init = (jnp.full((2 * t, 1), NEG_INF, F32), jnp.zeros((2 * t, 1), F32), jnp.zeros((2 * t, dv), F32))
    _, l, acc = lax.fori_loop(0, qi + 1, body, init)
    out = _diff_finish(acc / l, lq1_ref[...], lk1_ref[...], lq2_ref[...], lk2_ref[...], g_ref[...],
                       t, lam_init)
    o_ref[...] = out.astype(o_ref.dtype)


def _diff_sample_kernel(q_ref, kc_ref, vc_ref, kn_ref, vn_ref, bc_ref, bn_ref, lq1_ref, lk1_ref,
                        lq2_ref, lk2_ref, g_ref, o_ref, *, t, d_a, lam_init):
    qq = _split_maps(q_ref[...], d_a)
    scale = d_a ** -0.5
    bc = bc_ref[...]
    bn = bn_ref[...]
    s_c = _nt_dot(qq, kc_ref[...].astype(BF16)) * scale + jnp.concatenate([bc, bc], axis=0)
    s_n = _nt_dot(qq, kn_ref[...]) * scale + jnp.concatenate([bn, bn], axis=0)
    m = jnp.maximum(jnp.max(s_c, axis=1, keepdims=True), jnp.max(s_n, axis=1, keepdims=True))
    p_c = jnp.exp(s_c - m)
    p_n = jnp.exp(s_n - m)
    l = jnp.sum(p_c, axis=1, keepdims=True) + jnp.sum(p_n, axis=1, keepdims=True)
    acc = (jnp.dot(p_c.astype(BF16), vc_ref[...].astype(BF16), preferred_element_type=F32)
           + jnp.dot(p_n.astype(BF16), vn_ref[...], preferred_element_type=F32))
    out = _diff_finish(acc / l, lq1_ref[...], lk1_ref[...], lq2_ref[...], lk2_ref[...], g_ref[...],
                       t, lam_init)
    o_ref[...] = out.astype(o_ref.dtype)


def _sb_block(q, kb, vb, r_sum, acc, scale, causal):
    tk = kb.shape[0]
    z = _nt_dot(q, kb) * scale
    sp = jnp.log(1.0 + jnp.exp(-jnp.abs(z)))
    ls = jnp.minimum(z, 0.0) - sp
    l1m = ls - z
    if causal is not None:
        l1m = jnp.where(causal, l1m, 0.0)
    row = lax.broadcasted_iota(jnp.int32, (tk, tk), 0)
    col = lax.broadcasted_iota(jnp.int32, (tk, tk), 1)
    tri = (row > col).astype(BF16)
    hi = l1m.astype(BF16)
    lo = (l1m - hi.astype(F32)).astype(BF16)
    between = (jnp.dot(hi, tri, preferred_element_type=F32)
               + jnp.dot(lo, tri, preferred_element_type=F32)) + r_sum
    a = jnp.exp(ls + between)
    if causal is not None:
        a = jnp.where(causal, a, 0.0)
    acc = acc + jnp.dot(a.astype(BF16), vb, preferred_element_type=F32)
    r_sum = r_sum + jnp.sum(l1m, axis=1, keepdims=True)
    return r_sum, acc


def _strict_causal(tq, tk):
    row = lax.broadcasted_iota(jnp.int32, (tq, tk), 0)
    col = lax.broadcasted_iota(jnp.int32, (tq, tk), 1)
    return col < row


def _sb_prompt_kernel(q_ref, k_ref, v_ref, o_ref, *, t, d_b):
    qi = pl.program_id(2)
    q = q_ref[...]
    scale = d_b ** -0.5
    dv = v_ref.shape[-1]
    start = pl.multiple_of(qi * t, t)
    carry = _sb_block(q, k_ref[pl.ds(start, t), :], v_ref[pl.ds(start, t), :],
                      jnp.zeros((t, 1), F32), jnp.zeros((t, dv), F32), scale, _strict_causal(t, t))

    def body(n, carry):
        s = pl.multiple_of((qi - 1 - n) * t, t)
        return _sb_block(q, k_ref[pl.ds(s, t), :], v_ref[pl.ds(s, t), :], carry[0], carry[1],
                         scale, None)

    _, acc = lax.fori_loop(0, qi, body, carry)
    o_ref[...] = acc.astype(o_ref.dtype)


def _sb_sample_kernel(q_ref, kc_ref, vc_ref, kn_ref, vn_ref, o_ref, *, t, d_b, tc):
    q = q_ref[...]
    scale = d_b ** -0.5
    dv = vn_ref.shape[-1]
    past = kc_ref.shape[0]
    carry = _sb_block(q, kn_ref[...], vn_ref[...], jnp.zeros((t, 1), F32), jnp.zeros((t, dv), F32),
                      scale, _strict_causal(t, t))

    def body(n, carry):
        s = pl.multiple_of(past - (n + 1) * tc, tc)
        return _sb_block(q, kc_ref[pl.ds(s, tc), :].astype(BF16), vc_ref[pl.ds(s, tc), :].astype(BF16),
                         carry[0], carry[1], scale, None)

    _, acc = lax.fori_loop(0, past // tc, body, carry)
    o_ref[...] = acc.astype(o_ref.dtype)


def _vec_spec(n, grid_rank):
    return pl.BlockSpec((1, n), lambda *_: (0, 0))


def _prompt_attention(q, kv, bias_tiles, lam_vecs, subln_g, batch, seq, heads, d_a, d_b, lam_init):
    t = MAX_DISTANCE
    nq = seq // t
    grid = (batch, heads, nq)
    w2 = 2 * d_a
    q_spec = lambda off: pl.BlockSpec((t, w2), lambda b, h, i: (b * nq + i, off + h))
    kv_spec = lambda off: pl.BlockSpec((seq, w2), lambda b, h, i: (b, off + h))
    o_spec = pl.BlockSpec((t, w2), lambda b, h, i: (b * nq + i, h))
    o_shape = jax.ShapeDtypeStruct((batch * seq, heads * w2), BF16)
    vec = pl.BlockSpec((1, d_a), lambda b, h, i: (0, 0))
    oa = pl.pallas_call(
        functools.partial(_diff_prompt_kernel, t=t, d_a=d_a, lam_init=lam_init),
        grid=grid,
        in_specs=[q_spec(0), kv_spec(0), kv_spec(heads),
                  pl.BlockSpec((None, 3, t, t), lambda b, h, i: (h, 0, 0, 0)),
                  vec, vec, vec, vec,
                  pl.BlockSpec((1, w2), lambda b, h, i: (0, 0))],
        out_specs=o_spec, out_shape=o_shape,
        compiler_params=_params("parallel", "parallel", "parallel"),
        name="diff_attn_prompt",
    )(q, kv, kv, bias_tiles, *lam_vecs, subln_g)
    assert d_b == w2
    ob = pl.pallas_call(
        functools.partial(_sb_prompt_kernel, t=t, d_b=d_b),
        grid=grid,
        in_specs=[q_spec(heads), kv_spec(2 * heads), kv_spec(3 * heads)],
        out_specs=o_spec, out_shape=o_shape,
        compiler_params=_params("parallel", "parallel", "parallel"),
        name="sb_attn_prompt",
    )(q, kv, kv)
    return oa, ob


def _sample_attention(q, kv, caches, bias_c, bias_n, lam_vecs, subln_g, heads, d_a, d_b, lam_init):
    db, ts, _ = q.shape
    past = caches[0].shape[1]
    w2 = 2 * d_a
    grid = (db, heads)
    q_spec = lambda off: pl.BlockSpec((None, ts, w2), lambda b, h: (b, 0, off + h))
    c_spec = pl.BlockSpec((None, past, w2), lambda b, h: (b, 0, h))
    o_spec = pl.BlockSpec((None, ts, w2), lambda b, h: (b, 0, h))
    o_shape = jax.ShapeDtypeStruct((db, ts, heads * w2), BF16)
    vec = pl.BlockSpec((1, d_a), lambda b, h: (0, 0))
    oa = pl.pallas_call(
        functools.partial(_diff_sample_kernel, t=ts, d_a=d_a, lam_init=lam_init),
        grid=grid,
        in_specs=[q_spec(0), c_spec, c_spec, q_spec(0), q_spec(heads),
                  pl.BlockSpec((None, ts, past), lambda b, h: (h, 0, 0)),
                  pl.BlockSpec((None, ts, ts), lambda b, h: (h, 0, 0)),
                  vec, vec, vec, vec,
                  pl.BlockSpec((1, w2), lambda b, h: (0, 0))],
        out_specs=o_spec, out_shape=o_shape,
        compiler_params=_params("parallel", "parallel"),
        name="diff_attn_sample",
    )(q, caches[0], caches[1], kv, kv, bias_c, bias_n, *lam_vecs, subln_g)
    tc = _row_tile(past, 256)
    ob = pl.pallas_call(
        functools.partial(_sb_sample_kernel, t=ts, d_b=d_b, tc=tc),
        grid=grid,
        in_specs=[q_spec(heads), c_spec, c_spec, q_spec(2 * heads), q_spec(3 * heads)],
        out_specs=o_spec, out_shape=o_shape,
        compiler_params=_params("parallel", "parallel"),
        name="sb_attn_sample",
    )(q, caches[2], caches[3], kv, kv)
    return oa.reshape(db * ts, heads * w2), ob.reshape(db * ts, heads * w2)


def _merge_kernel(oa_ref, ob_ref, g_ref, wa_ref, wb_ref, o_ref, *, d):
    a = jnp.dot(oa_ref[...], wa_ref[...], preferred_element_type=F32)
    b = jnp.dot(ob_ref[...], wb_ref[...], preferred_element_type=F32)
    ga = jax.nn.sigmoid(g_ref[:, :d])
    gb = jax.nn.sigmoid(g_ref[:, d:])
    o_ref[...] = (ga * a + gb * b).astype(o_ref.dtype)


def _merge(oa, ob, gates, wa, wb):
    n, w = oa.shape
    d = wa.shape[1]
    tm = _row_tile(n, 256)
    return pl.pallas_call(
        functools.partial(_merge_kernel, d=d),
        grid=(n // tm,),
        in_specs=[pl.BlockSpec((tm, w), lambda i: (i, 0)),
                  pl.BlockSpec((tm, w), lambda i: (i, 0)),
                  pl.BlockSpec((tm, 2 * d), lambda i: (i, 0)),
                  pl.BlockSpec((w, d), lambda i: (0, 0)),
                  pl.BlockSpec((w, d), lambda i: (0, 0))],
        out_specs=pl.BlockSpec((tm, d), lambda i: (i, 0)),
        out_shape=jax.ShapeDtypeStruct((n, d), BF16),
        compiler_params=_params("parallel"),
        name="branch_merge",
    )(oa, ob, gates, wa, wb)


def _ffn_kernel(te_ref, tv_ref, x_ref, wg_ref, wu_ref, wd_ref, *rest, has_res):
    if has_res:
        res_ref, o_ref, xb_ref, acc_ref = rest
    else:
        res_ref = None
        o_ref, xb_ref, acc_ref = rest
    i = pl.program_id(0)
    j = pl.program_id(1)
    last = pl.num_programs(1) - 1
    valid = tv_ref[i] > 0

    @pl.when(jnp.logical_and(valid, j == 0))
    def _():
        xb_ref[...] = x_ref[...].astype(BF16)
        acc_ref[...] = jnp.zeros_like(acc_ref)

    @pl.when(valid)
    def _():
        x = xb_ref[...]
        g = jnp.dot(x, wg_ref[...], preferred_element_type=F32)
        u = jnp.dot(x, wu_ref[...], preferred_element_type=F32)
        a = (g * jax.nn.sigmoid(g)) * u
        acc_ref[...] += jnp.dot(a.astype(BF16), wd_ref[...], preferred_element_type=F32)

    @pl.when(jnp.logical_and(valid, j == last))
    def _():
        out = acc_ref[...]
        if has_res:
            out = out + res_ref[...]
        o_ref[...] = out

    @pl.when(jnp.logical_and(jnp.logical_not(valid), j == last))
    def _():
        o_ref[...] = jnp.zeros_like(o_ref)


def _ffn(x, wg, wu, wd, tile_expert, tile_valid, tm, name, res=None):
    n, d = x.shape
    f = wg.shape[2]
    tf = _row_tile(f, 512)
    nj = f // tf

    def fj(i, j, tv):
        return jnp.where(tv[i] > 0, j, nj - 1)

    in_specs = [pl.BlockSpec((tm, d), lambda i, j, te, tv: (i, 0)),
                pl.BlockSpec((None, d, tf), lambda i, j, te, tv: (te[i], 0, fj(i, j, tv))),
                pl.BlockSpec((None, d, tf), lambda i, j, te, tv: (te[i], 0, fj(i, j, tv))),
                pl.BlockSpec((None, tf, d), lambda i, j, te, tv: (te[i], fj(i, j, tv), 0))]
    args = [x, wg, wu, wd]
    if res is not None:
        in_specs.append(pl.BlockSpec((tm, d), lambda i, j, te, tv: (i, 0)))
        args.append(res)
    return pl.pallas_call(
        functools.partial(_ffn_kernel, has_res=res is not None),
        grid_spec=pltpu.PrefetchScalarGridSpec(
            num_scalar_prefetch=2,
            grid=(n // tm, nj),
            in_specs=in_specs,
            out_specs=pl.BlockSpec((tm, d), lambda i, j, te, tv: (i, 0)),
            scratch_shapes=[pltpu.VMEM((tm, d), BF16), pltpu.VMEM((tm, d), F32)]),
        out_shape=jax.ShapeDtypeStruct((n, d), F32),
        compiler_params=_params("parallel", "arbitrary"),
        name=name,
    )(tile_expert, tile_valid, *args)


def _router_kernel(h_ref, w_ref, b_ref, idx_ref, gate_ref, cnt_ref, run_ref, *, tm):
    i = pl.program_id(0)

    @pl.when(i == 0)
    def _():
        run_ref[...] = jnp.zeros_like(run_ref)

    logits = jnp.dot(h_ref[...], w_ref[...], preferred_element_type=F32) + b_ref[...]
    lane = lax.broadcasted_iota(jnp.int32, logits.shape, 1)
    m1 = jnp.max(logits, axis=1, keepdims=True)
    i1 = jnp.min(jnp.where(logits == m1, lane, LANES), axis=1, keepdims=True)
    oh1 = lane == i1
    rest = jnp.where(oh1, -jnp.inf, logits)
    m2 = jnp.max(rest, axis=1, keepdims=True)
    i2 = jnp.min(jnp.where(rest == m2, lane, LANES), axis=1, keepdims=True)
    oh2 = lane == i2
    e = jnp.exp(m2 - m1)
    g1 = 1.0 / (1.0 + e)
    g2 = e / (1.0 + e)

    row = lax.broadcasted_iota(jnp.int32, (tm, tm), 0)
    col = lax.broadcasted_iota(jnp.int32, (tm, tm), 1)
    tri = (col < row).astype(BF16)
    picked = jnp.logical_or(oh1, oh2).astype(BF16)
    before = jnp.dot(tri, picked, preferred_element_type=F32) + run_ref[...]
    r1 = jnp.sum(jnp.where(oh1, before, 0.0), axis=1, keepdims=True).astype(jnp.int32)
    r2 = jnp.sum(jnp.where(oh2, before, 0.0), axis=1, keepdims=True).astype(jnp.int32)
    run_ref[...] += jnp.sum(picked.astype(F32), axis=0, keepdims=True)

    idx_ref[...] = jnp.where(lane == 0, i1, jnp.where(lane == 1, i2,
                             jnp.where(lane == 2, r1, jnp.where(lane == 3, r2, 0))))
    gate_ref[...] = jnp.where(lane == 0, g1, jnp.where(lane == 1, g2, 0.0))
    cnt_ref[...] = run_ref[...]


def _router(h, w_r, b_r):
    n, d = h.shape
    ne = w_r.shape[1]
    tm = _row_tile(n, 512)
    w_pad = jnp.zeros((d, LANES), BF16).at[:, :ne].set(w_r.astype(BF16))
    b_pad = jnp.full((1, LANES), NEG_INF, F32).at[0, :ne].set(b_r.astype(F32))
    return pl.pallas_call(
        functools.partial(_router_kernel, tm=tm),
        grid=(n // tm,),
        in_specs=[pl.BlockSpec((tm, d), lambda i: (i, 0)),
                  pl.BlockSpec((d, LANES), lambda i: (0, 0)),
                  pl.BlockSpec((1, LANES), lambda i: (0, 0))],
        out_specs=[pl.BlockSpec((tm, LANES), lambda i: (i, 0)),
                   pl.BlockSpec((tm, LANES), lambda i: (i, 0)),
                   pl.BlockSpec((1, LANES), lambda i: (0, 0))],
        out_shape=[jax.ShapeDtypeStruct((n, LANES), jnp.int32),
                   jax.ShapeDtypeStruct((n, LANES), F32),
                   jax.ShapeDtypeStruct((1, LANES), F32)],
        scratch_shapes=[pltpu.VMEM((1, LANES), F32)],
        compiler_params=_params("arbitrary"),
        name="moe_router",
    )(h, w_pad, b_pad)


def _row_copy(src, src_row, dst, dst_row, sem):
    return pltpu.make_async_copy(src.at[pl.ds(src_row, 1), :], dst.at[pl.ds(dst_row, 1), :], sem)


def _dispatch_kernel(pos_ref, h_ref, xs_in_ref, xs_ref, sem, *, tt):
    del xs_in_ref
    base = pl.program_id(0) * tt

    def issue(r, c):
        for k in range(TOP_K):
            _row_copy(h_ref, base + r, xs_ref, pos_ref[0, 0, TOP_K * r + k], sem).start()
        return c

    lax.fori_loop(0, tt, issue, 0)

    def drain(r, c):
        for k in range(TOP_K):
            _row_copy(h_ref, 0, xs_ref, 0, sem).wait()
        return c

    lax.fori_loop(0, tt, drain, 0)


def _dispatch(h, pos, n_slots):
    n, d = h.shape
    tt = _row_tile(n, 256)
    pos3 = pos.reshape(n // tt, 1, TOP_K * tt)
    return pl.pallas_call(
        functools.partial(_dispatch_kernel, tt=tt),
        grid=(n // tt,),
        in_specs=[pl.BlockSpec((1, 1, TOP_K * tt), lambda i: (i, 0, 0), memory_space=pltpu.SMEM),
                  pl.BlockSpec(memory_space=pl.ANY),
                  pl.BlockSpec(memory_space=pl.ANY)],
        out_specs=pl.BlockSpec(memory_space=pl.ANY),
        out_shape=jax.ShapeDtypeStruct((n_slots, d), h.dtype),
        scratch_shapes=[pltpu.SemaphoreType.DMA(())],
        input_output_aliases={2: 0},
        compiler_params=_params("arbitrary"),
        name="moe_dispatch",
    )(pos3, h, jnp.zeros((n_slots, d), h.dtype))


def _combine_kernel(pos_ref, x_ref, gate_ref, y_ref, o_ref, buf_ref, sem, *, tt):
    def issue(r, c):
        for k in range(TOP_K):
            _row_copy(y_ref, pos_ref[0, 0, TOP_K * r + k], buf_ref.at[k], r, sem).start()
        return c

    lax.fori_loop(0, tt, issue, 0)

    def drain(r, c):
        for k in range(TOP_K):
            _row_copy(y_ref, 0, buf_ref.at[k], 0, sem).wait()
        return c

    lax.fori_loop(0, tt, drain, 0)
    g = gate_ref[...]
    o_ref[...] = x_ref[...] + (g[:, 0:1] * buf_ref[0] + g[:, 1:2] * buf_ref[1])


def _combine(x, gates, y, pos):
    n, d = x.shape
    tt = _row_tile(n, 256)
    pos3 = pos.reshape(n // tt, 1, TOP_K * tt)
    return pl.pallas_call(
        functools.partial(_combine_kernel, tt=tt),
        grid=(n // tt,),
        in_specs=[pl.BlockSpec((1, 1, TOP_K * tt), lambda i: (i, 0, 0), memory_space=pltpu.SMEM),
                  pl.BlockSpec((tt, d), lambda i: (i, 0)),
                  pl.BlockSpec((tt, LANES), lambda i: (i, 0)),
                  pl.BlockSpec(memory_space=pl.ANY)],
        out_specs=pl.BlockSpec((tt, d), lambda i: (i, 0)),
        out_shape=jax.ShapeDtypeStruct((n, d), F32),
        scratch_shapes=[pltpu.VMEM((TOP_K, tt, d), F32), pltpu.SemaphoreType.DMA(())],
        compiler_params=_params("arbitrary"),
        name="moe_combine",
    )(pos3, x, gates, y)


def _moe(x, h_bf, h_f32, w_r, b_r, wg, wu, wd, tm):
    n, d = x.shape
    ne = wg.shape[0]
    idx, gates, counts = _router(h_bf, w_r, b_r)
    counts = counts[0, :ne].astype(jnp.int32)
    padded = ((counts + tm - 1) // tm) * tm
    ends = jnp.cumsum(padded)
    offsets = ends - padded
    pos = offsets[idx[:, :TOP_K]] + idx[:, TOP_K:2 * TOP_K]
    n_tiles = (n * TOP_K) // tm + ne
    tile_start = jnp.arange(n_tiles, dtype=jnp.int32) * tm
    tile_valid = (tile_start < ends[-1]).astype(jnp.int32)
    tile_expert = jnp.minimum(jnp.sum((tile_start[:, None] >= ends[None, :]).astype(jnp.int32), axis=1),
                              ne - 1)
    last_expert = jnp.max(jnp.where(counts > 0, jnp.arange(ne, dtype=jnp.int32), 0))
    tile_expert = jnp.where(tile_valid > 0, tile_expert, last_expert)
    xs = _dispatch(h_f32, pos, n_tiles * tm)
    y = _ffn(xs, wg, wu, wd, tile_expert, tile_valid, tm, "moe_ffn")
    return _combine(x, gates, y, pos)


def kernel(x_prompt, x_sample, cache_diff_k, cache_diff_v, cache_sb_k, cache_sb_v, rel_bias, norm_mix_g, w_in, lambda_q1, lambda_k1, lambda_q2, lambda_k2, subln_g, w_branch_a, w_branch_b, w_out, norm_ffn_g, dense_w_gate, dense_w_up, dense_w_down, router_w, router_b, moe_w_gate, moe_w_up, moe_w_down, final_norm_g):
    batch, seq, d = x_prompt.shape
    db, ts, _ = x_sample.shape
    depth, _, past, heads, w2 = cache_diff_k.shape
    d_a = w2 // 2
    d_b = cache_sb_k.shape[-1]
    wid = heads * w2
    n_p = batch * seq
    n_s = db * ts
    assert d % wid == 0 and w_in.shape[2] == 6 * wid + 2 * d
    assert cache_sb_k.shape[3] * d_b == wid and cache_diff_v.shape[-1] == w2

    x = jnp.concatenate([x_prompt.reshape(n_p, d), x_sample.reshape(n_s, d)], axis=0)

    pos_q = past + jnp.arange(ts, dtype=jnp.int32)
    bias_tiles = _prompt_bias_tiles(rel_bias, MAX_DISTANCE)
    bias_c = _masked_bias(rel_bias, pos_q, jnp.arange(past, dtype=jnp.int32))
    bias_n = _masked_bias(rel_bias, pos_q, pos_q)

    kv_rows = []
    for l in range(depth):
        lam_init = 0.8 - 0.6 * math.exp(-0.3 * l)
        w_l = w_in[l].astype(BF16)
        (h,) = _rmsnorm(x, norm_mix_g[l], [BF16], "norm_mix")
        (q,) = _matmul(h, w_l, lambda j: 3 * j, 2, wid, [BF16], "proj_q")
        kv32, kv16 = _matmul(h, w_l, lambda j: jnp.where(j >= 2, j + 2, j + 1), 4, wid, [F32, BF16],
                             "proj_kv")
        (gates,) = _matmul(h, w_l, lambda j: j + 6, 2 * d // wid, wid, [F32], "proj_gate")
        kv_rows.append(kv32)

        lam_vecs = [v[l].reshape(1, d_a).astype(F32) for v in (lambda_q1, lambda_k1, lambda_q2, lambda_k2)]
        g_sub = subln_g[l].reshape(1, w2).astype(F32)
        oa_p, ob_p = _prompt_attention(q, kv16, bias_tiles, lam_vecs, g_sub, batch, seq, heads, d_a, d_b,
                                       lam_init)
        caches = [c[l].reshape(db, past, wid) for c in (cache_diff_k, cache_diff_v, cache_sb_k, cache_sb_v)]
        oa_s, ob_s = _sample_attention(q[n_p:].reshape(db, ts, 2 * wid), kv16[n_p:].reshape(db, ts, 4 * wid),
                                       caches, bias_c, bias_n, lam_vecs, g_sub, heads, d_a, d_b, lam_init)
        oa = jnp.concatenate([oa_p, oa_s], axis=0)
        ob = jnp.concatenate([ob_p, ob_s], axis=0)
        merged = _merge(oa, ob, gates, w_branch_a[l].astype(BF16), w_branch_b[l].astype(BF16))
        (x,) = _matmul(merged, w_out[l].astype(BF16), lambda j: j, 1, d, [F32], "proj_out", res=x)

        if l % 2 == 0:
            j = l // 2
            (h2,) = _rmsnorm(x, norm_ffn_g[l], [BF16], "norm_ffn")
            tm = _row_tile(x.shape[0], 512)
            n_tiles = x.shape[0] // tm
            x = _ffn(h2, dense_w_gate[j:j + 1].astype(BF16), dense_w_up[j:j + 1].astype(BF16),
                     dense_w_down[j:j + 1].astype(BF16), jnp.zeros((n_tiles,), jnp.int32),
                     jnp.ones((n_tiles,), jnp.int32), tm, "dense_ffn", res=x)
        else:
            j = l // 2
            h2, h2_f32 = _rmsnorm(x, norm_ffn_g[l], [BF16, F32], "norm_ffn")
            x = _moe(x, h2, h2_f32, router_w[j], router_b[j], moe_w_gate[j].astype(BF16),
                     moe_w_up[j].astype(BF16), moe_w_down[j].astype(BF16), _row_tile(x.shape[0], 512))

    (y,) = _rmsnorm(x, final_norm_g, [F32], "norm_final")
    y_prompt = y[:n_p].reshape(batch, seq, d)
    y_sample = y[n_p:].reshape(db, ts, d)

    def rows(col, lo, hi, shape):
        return jnp.stack([kv[lo:hi, col * wid:(col + 1) * wid].reshape(shape) for kv in kv_rows])

    outs_p = [rows(c, 0, n_p, (batch, seq, heads, w2)) for c in range(4)]
    outs_s = [rows(c, n_p, n_p + n_s, (db, ts, heads, w2)) for c in range(4)]
    return (y_prompt, y_sample, *outs_p, *outs_s)
```

```python
import functools
import math

import jax
import jax.numpy as jnp
from jax import lax
from jax.experimental import pallas as pl
from jax.experimental.pallas import tpu as pltpu

CHUNK = 64
NUM_BUCKETS = 32
MAX_DISTANCE = 128
TOP_K = 2
EPS = 1e-6
NEG_INF = -1e30

LANES = 128
VMEM_LIMIT_BYTES = 56 * 1024 * 1024
TOKEN_TILE = 512
ATTN_TILE = 512
FF_TILE = 512
ROW_DMA_TILE = 256

BF16 = jnp.bfloat16
F32 = jnp.float32


def _params(*sem):
    return pltpu.CompilerParams(dimension_semantics=sem, vmem_limit_bytes=VMEM_LIMIT_BYTES)


def _tile(n, want):
    t = min(n, want)
    while n % t:
        t //= 2
    return t


def _rms(x, g):
    return x * lax.rsqrt(jnp.mean(x * x, axis=-1, keepdims=True) + EPS) * g


def _rmsnorm_kernel(x_ref, g_ref, *o_refs):
    y = _rms(x_ref[...].astype(F32), g_ref[...])
    for o_ref in o_refs:
        o_ref[...] = y.astype(o_ref.dtype)


def _rmsnorm(x, g, out_dtypes, name, row0=0, n_rows=None):
    d = x.shape[1]
    n_rows = x.shape[0] if n_rows is None else n_rows
    tm = _tile(n_rows, TOKEN_TILE)
    assert row0 % tm == 0
    off = row0 // tm
    return pl.pallas_call(
        _rmsnorm_kernel,
        grid=(n_rows // tm,),
        in_specs=[pl.BlockSpec((tm, d), lambda i: (off + i, 0)),
                  pl.BlockSpec((1, d), lambda i: (0, 0))],
        out_specs=[pl.BlockSpec((tm, d), lambda i: (i, 0)) for _ in out_dtypes],
        out_shape=[jax.ShapeDtypeStruct((n_rows, d), dt) for dt in out_dtypes],
        compiler_params=_params("parallel"),
        name=name,
    )(x, g.reshape(1, d).astype(F32))


def _proj_kernel(x_ref, w_ref, *rest, n_out):
    o_refs, wb_ref = rest[-1 - n_out:-1], rest[-1]

    @pl.when(pl.program_id(1) == 0)
    def _():
        wb_ref[...] = w_ref[...].astype(BF16)

    acc = jnp.dot(x_ref[...], wb_ref[...], preferred_element_type=F32)
    for o_ref in o_refs:
        o_ref[...] = acc.astype(o_ref.dtype)


def _proj(x, w, layer, col_block_fn, n_blocks, tn, outs, name, row0=0, n_rows=None):
    k = x.shape[1]
    n_rows = x.shape[0] if n_rows is None else n_rows
    tm = _tile(n_rows, TOKEN_TILE)
    assert row0 % tm == 0
    off = row0 // tm
    aliased = [o[4] for o in outs if o[4] is not None]
    aliases = {}
    for n, o in enumerate(outs):
        if o[4] is not None:
            aliases[2 + len(aliases)] = n
    return pl.pallas_call(
        functools.partial(_proj_kernel, n_out=len(outs)),
        grid=(n_blocks, n_rows // tm),
        in_specs=[pl.BlockSpec((tm, k), lambda j, i: (off + i, 0)),
                  pl.BlockSpec((None, k, tn), lambda j, i: (layer, 0, col_block_fn(j)))]
                 + [pl.BlockSpec(memory_space=pl.ANY) for _ in aliased],
        out_specs=[pl.BlockSpec(o[2], o[3]) for o in outs],
        out_shape=[jax.ShapeDtypeStruct(o[0], o[1]) for o in outs],
        scratch_shapes=[pltpu.VMEM((k, tn), BF16)],
        input_output_aliases=aliases,
        compiler_params=_params("parallel", "arbitrary"),
        name=name,
    )(x, w, *aliased)


def _proj_cache_rows(h, w_in, layer, col, wid, row0, n_rows, prev, name):
    depth = w_in.shape[0]
    tm = _tile(n_rows, TOKEN_TILE)
    if prev is None:
        prev = jnp.zeros((depth, n_rows, wid), F32)
    outs = [((depth, n_rows, wid), F32, (None, tm, wid), lambda j, i: (layer, i, 0), prev),
            ((n_rows, wid), BF16, (tm, wid), lambda j, i: (i, 0), None)]
    return _proj(h, w_in, layer, lambda j: col, 1, wid, outs, name, row0, n_rows)


def _t5_bucket(rel):
    half = NUM_BUCKETS // 2
    max_exact = half // 2
    n = jnp.abs(rel)
    far = max_exact + (jnp.log(jnp.maximum(n, 1).astype(F32) / max_exact)
                       / math.log(MAX_DISTANCE / max_exact) * (half - max_exact)).astype(jnp.int32)
    far = jnp.minimum(far, half - 1)
    return jnp.where(rel > 0, half, 0) + jnp.where(n < max_exact, n, far)


def _masked_bias(rel_bias, q_pos, k_pos):
    bucket = _t5_bucket(k_pos[None, :] - q_pos[:, None])
    bias = jnp.zeros((rel_bias.shape[1],) + bucket.shape, F32)
    for b in range(NUM_BUCKETS):
        bias = jnp.where(bucket[None] == b, rel_bias[b].astype(F32)[:, None, None], bias)
    mask = (k_pos[None, :] // CHUNK) <= (q_pos[:, None] // CHUNK)
    return jnp.where(mask[None], bias, NEG_INF)


def _prompt_bias_tiles(rel_bias, t):
    assert t >= MAX_DISTANCE and t % CHUNK == 0
    pos = jnp.arange(t, dtype=jnp.int32)
    tiles = [_masked_bias(rel_bias, 2 * t + pos, (2 - d) * t + pos) for d in (2, 1, 0)]
    return jnp.stack(tiles, axis=1)


def _split_maps(q, d_a):
    lane = lax.broadcasted_iota(jnp.int32, q.shape, 1)
    zero = jnp.zeros_like(q)
    return jnp.concatenate([jnp.where(lane < d_a, q, zero), jnp.where(lane >= d_a, q, zero)], axis=0)


def _lambda(lq1, lk1, lq2, lk2, lam_init):
    return (jnp.exp(jnp.sum(lq1 * lk1, axis=-1, keepdims=True))
            - jnp.exp(jnp.sum(lq2 * lk2, axis=-1, keepdims=True)) + lam_init)


def _diff_finish(o, lam, g, t, lam_init):
    oa = o[:t] - lam * o[t:]
    return _rms(oa, g) * (1.0 - lam_init)


def _nt_dot(a, b):
    return lax.dot_general(a, b, (((1,), (1,)), ((), ())), preferred_element_type=F32)


def _diff_prompt_kernel(q_ref, k_ref, v_ref, bias_ref, lq1_ref, lk1_ref, lq2_ref, lk2_ref, g_ref,
                        o_in_ref, o_ref, *, t, d_a, lam_init):
    del o_in_ref
    qi = pl.program_id(2)
    qq = _split_maps(q_ref[...], d_a)
    scale = d_a ** -0.5
    dv = v_ref.shape[-1]

    def body(j, carry):
        m, l, acc = carry
        start = pl.multiple_of(j * t, t)
        kb = k_ref[pl.ds(start, t), :]
        vb = v_ref[pl.ds(start, t), :]
        bt = bias_ref[jnp.clip(j - qi + 2, 0, 2)]
        s = _nt_dot(qq, kb) * scale + jnp.concatenate([bt, bt], axis=0)
        m_new = jnp.maximum(m, jnp.max(s, axis=1, keepdims=True))
        alpha = jnp.exp(m - m_new)
        p = jnp.exp(s - m_new)
        l = alpha * l + jnp.sum(p, axis=1, keepdims=True)
        acc = alpha * acc + jnp.dot(p.astype(BF16), vb, preferred_element_type=F32)
        return m_new, l, acc

    init = (jnp.full((2 * t, 1), NEG_INF, F32), jnp.zeros((2 * t, 1), F32), jnp.zeros((2 * t, dv), F32))
    _, l, acc = lax.fori_loop(0, qi + 1, body, init)
    lam = _lambda(lq1_ref[...], lk1_ref[...], lq2_ref[...], lk2_ref[...], lam_init)
    o_ref[...] = _diff_finish(acc / l, lam, g_ref[...], t, lam_init).astype(o_ref.dtype)


def _diff_sample_kernel(q_ref, kc_ref, vc_ref, kn_ref, vn_ref, bc_ref, bn_ref, lq1_ref, lk1_ref,
                        lq2_ref, lk2_ref, g_ref, o_in_ref, o_ref, *, t, d_a, lam_init, hp):
    del o_in_ref
    scale = d_a ** -0.5
    w2 = 2 * d_a
    lam = _lambda(lq1_ref[...], lk1_ref[...], lq2_ref[...], lk2_ref[...], lam_init)
    for h in range(hp):
        cols = slice(h * w2, (h + 1) * w2)
        qq = _split_maps(q_ref[:, cols], d_a)
        bc = bc_ref[h]
        bn = bn_ref[h]
        s_c = _nt_dot(qq, kc_ref[:, cols].astype(BF16)) * scale + jnp.concatenate([bc, bc], axis=0)
        s_n = _nt_dot(qq, kn_ref[:, cols]) * scale + jnp.concatenate([bn, bn], axis=0)
        m = jnp.maximum(jnp.max(s_c, axis=1, keepdims=True), jnp.max(s_n, axis=1, keepdims=True))
        p_c = jnp.exp(s_c - m)
        p_n = jnp.exp(s_n - m)
        l = jnp.sum(p_c, axis=1, keepdims=True) + jnp.sum(p_n, axis=1, keepdims=True)
        acc = (jnp.dot(p_c.astype(BF16), vc_ref[:, cols].astype(BF16), preferred_element_type=F32)
               + jnp.dot(p_n.astype(BF16), vn_ref[:, cols], preferred_element_type=F32))
        o_ref[:, cols] = _diff_finish(acc / l, lam, g_ref[...], t, lam_init).astype(o_ref.dtype)


def _triangle(n):
    row = lax.broadcasted_iota(jnp.int32, (n, n), 0)
    col = lax.broadcasted_iota(jnp.int32, (n, n), 1)
    return (row > col).astype(BF16)


def _suffix_sums(x, tri):
    hi = x.astype(BF16)
    lo = (x - hi.astype(F32)).astype(BF16)
    return jnp.dot(hi, tri, preferred_element_type=F32) + jnp.dot(lo, tri, preferred_element_type=F32)


def _sb_span(q, kb, vb, r_sum, acc, scale, causal, n_parts):
    tk = kb.shape[0]
    part = tk // n_parts
    z = _nt_dot(q, kb) * scale
    sp = jnp.log(1.0 + jnp.exp(-jnp.abs(z)))
    ls = jnp.minimum(z, 0.0) - sp
    l1m = ls - z
    if causal is not None:
        l1m = jnp.where(causal, l1m, 0.0)
    tri = _triangle(part)
    between = [None] * n_parts
    for n in reversed(range(n_parts)):
        x = l1m[:, n * part:(n + 1) * part]
        between[n] = _suffix_sums(x, tri) + r_sum
        r_sum = r_sum + jnp.sum(x, axis=1, keepdims=True)
    a = jnp.exp(ls + jnp.concatenate(between, axis=1))
    if causal is not None:
        a = jnp.where(causal, a, 0.0)
    acc = acc + jnp.dot(a.astype(BF16), vb, preferred_element_type=F32)
    return r_sum, acc


def _strict_causal(tq, tk):
    row = lax.broadcasted_iota(jnp.int32, (tq, tk), 0)
    col = lax.broadcasted_iota(jnp.int32, (tq, tk), 1)
    return col < row


def _sb_parts(t):
    return 2 if t % 256 == 0 else 1


def _sb_prompt_kernel(q_ref, k_ref, v_ref, o_in_ref, o_ref, *, t, d_b):
    del o_in_ref
    qi = pl.program_id(2)
    q = q_ref[...]
    scale = d_b ** -0.5
    dv = v_ref.shape[-1]
    parts = _sb_parts(t)
    start = pl.multiple_of(qi * t, t)
    carry = _sb_span(q, k_ref[pl.ds(start, t), :], v_ref[pl.ds(start, t), :],
                     jnp.zeros((t, 1), F32), jnp.zeros((t, dv), F32), scale, _strict_causal(t, t), parts)

    def body(n, carry):
        s = pl.multiple_of((qi - 1 - n) * t, t)
        return _sb_span(q, k_ref[pl.ds(s, t), :], v_ref[pl.ds(s, t), :], carry[0], carry[1],
                        scale, None, parts)

    _, acc = lax.fori_loop(0, qi, body, carry)
    o_ref[...] = acc.astype(o_ref.dtype)


def _sb_sample_kernel(q_ref, kc_ref, vc_ref, kn_ref, vn_ref, o_in_ref, o_ref, *, t, d_b, tc, hp):
    del o_in_ref
    scale = d_b ** -0.5
    past = kc_ref.shape[0]
    heads = [slice(h * d_b, (h + 1) * d_b) for h in range(hp)]
    qs = [q_ref[:, c] for c in heads]
    causal = _strict_causal(t, t)
    carry = tuple(_sb_span(qs[h], kn_ref[:, c], vn_ref[:, c], jnp.zeros((t, 1), F32),
                           jnp.zeros((t, d_b), F32), scale, causal, 1)
                  for h, c in enumerate(heads))

    def body(n, carry):
        s = pl.multiple_of(past - (n + 1) * tc, tc)
        return tuple(_sb_span(qs[h], kc_ref[pl.ds(s, tc), c].astype(BF16),
                              vc_ref[pl.ds(s, tc), c].astype(BF16), carry[h][0], carry[h][1],
                              scale, None, 1)
                     for h, c in enumerate(heads))

    carry = lax.fori_loop(0, past // tc, body, carry)
    for h, c in enumerate(heads):
        o_ref[:, c] = carry[h][1].astype(o_ref.dtype)


def _prompt_attention(q, kv, bias_tiles, lam_vecs, subln_g, n_tok, batch, seq, heads, d_a, d_b, t,
                      lam_init):
    nq = seq // t
    grid = (batch, heads, nq)
    w2 = 2 * d_a
    q_spec = lambda off: pl.BlockSpec((t, w2), lambda b, h, i: (b * nq + i, off + h))
    kv_spec = pl.BlockSpec((seq, w2), lambda b, h, i: (b, h))
    o_spec = pl.BlockSpec((t, w2), lambda b, h, i: (b * nq + i, h))
    o_shape = jax.ShapeDtypeStruct((n_tok, heads * w2), BF16)
    vec = pl.BlockSpec((1, d_a), lambda b, h, i: (0, 0))
    any_spec = pl.BlockSpec(memory_space=pl.ANY)
    blank = jnp.zeros(o_shape.shape, o_shape.dtype)
    oa = pl.pallas_call(
        functools.partial(_diff_prompt_kernel, t=t, d_a=d_a, lam_init=lam_init),
        grid=grid,
        in_specs=[q_spec(0), kv_spec, kv_spec,
                  pl.BlockSpec((None, 3, t, t), lambda b, h, i: (h, 0, 0, 0)),
                  vec, vec, vec, vec,
                  pl.BlockSpec((1, w2), lambda b, h, i: (0, 0)),
                  any_spec],
        out_specs=o_spec, out_shape=o_shape,
        input_output_aliases={9: 0},
        compiler_params=_params("parallel", "parallel", "parallel"),
        name="diff_attn_prompt",
    )(q, kv[0], kv[1], bias_tiles, *lam_vecs, subln_g, blank)
    assert d_b == w2
    ob = pl.pallas_call(
        functools.partial(_sb_prompt_kernel, t=t, d_b=d_b),
        grid=grid,
        in_specs=[q_spec(heads), kv_spec, kv_spec, any_spec],
        out_specs=o_spec, out_shape=o_shape,
        input_output_aliases={3: 0},
        compiler_params=_params("parallel", "parallel", "parallel"),
        name="sb_attn_prompt",
    )(q, kv[2], kv[3], blank)
    return oa, ob


def _sample_attention(q, kv, caches, bias_c, bias_n, lam_vecs, subln_g, oa, ob, n_p, db, ts, heads,
                      d_a, d_b, lam_init):
    past = caches[0].shape[1]
    w2 = 2 * d_a
    hp = 4 if heads % 4 == 0 else 1
    wg = hp * w2
    ng = heads // hp
    assert n_p % ts == 0
    row0 = n_p // ts
    grid = (db, ng)
    q_spec = lambda off: pl.BlockSpec((ts, wg), lambda b, g: (row0 + b, off + g))
    n_spec = pl.BlockSpec((ts, wg), lambda b, g: (b, g))
    c_spec = pl.BlockSpec((None, past, wg), lambda b, g: (b, 0, g))
    o_spec = pl.BlockSpec((ts, wg), lambda b, g: (row0 + b, g))
    any_spec = pl.BlockSpec(memory_space=pl.ANY)
    vec = pl.BlockSpec((1, d_a), lambda b, g: (0, 0))
    oa = pl.pallas_call(
        functools.partial(_diff_sample_kernel, t=ts, d_a=d_a, lam_init=lam_init, hp=hp),
        grid=grid,
        in_specs=[q_spec(0), c_spec, c_spec, n_spec, n_spec,
                  pl.BlockSpec((hp, ts, past), lambda b, g: (g, 0, 0)),
                  pl.BlockSpec((hp, ts, ts), lambda b, g: (g, 0, 0)),
                  vec, vec, vec, vec,
                  pl.BlockSpec((1, w2), lambda b, g: (0, 0)),
                  any_spec],
        out_specs=o_spec, out_shape=jax.ShapeDtypeStruct(oa.shape, oa.dtype),
        input_output_aliases={12: 0},
        compiler_params=_params("parallel", "parallel"),
        name="diff_attn_sample",
    )(q, caches[0], caches[1], kv[0], kv[1], bias_c, bias_n, *lam_vecs, subln_g, oa)
    tc = _tile(past, 256)
    ob = pl.pallas_call(
        functools.partial(_sb_sample_kernel, t=ts, d_b=d_b, tc=tc, hp=hp),
        grid=grid,
        in_specs=[q_spec(ng), c_spec, c_spec, n_spec, n_spec, any_spec],
        out_specs=o_spec, out_shape=jax.ShapeDtypeStruct(ob.shape, ob.dtype),
        input_output_aliases={5: 0},
        compiler_params=_params("parallel", "parallel"),
        name="sb_attn_sample",
    )(q, caches[2], caches[3], kv[2], kv[3], ob)
    return oa, ob


def _merge_kernel(oa_ref, ob_ref, g_ref, wa_ref, wb_ref, o_ref, *, d):
    a = jnp.dot(oa_ref[...], wa_ref[...], preferred_element_type=F32)
    b = jnp.dot(ob_ref[...], wb_ref[...], preferred_element_type=F32)
    ga = jax.nn.sigmoid(g_ref[:, :d])
    gb = jax.nn.sigmoid(g_ref[:, d:])
    o_ref[...] = (ga * a + gb * b).astype(o_ref.dtype)


def _merge(oa, ob, gates, wa, wb):
    n, w = oa.shape
    d = wa.shape[1]
    tm = _tile(n, TOKEN_TILE // 2)
    return pl.pallas_call(
        functools.partial(_merge_kernel, d=d),
        grid=(n // tm,),
        in_specs=[pl.BlockSpec((tm, w), lambda i: (i, 0)),
                  pl.BlockSpec((tm, w), lambda i: (i, 0)),
                  pl.BlockSpec((tm, 2 * d), lambda i: (i, 0)),
                  pl.BlockSpec((w, d), lambda i: (0, 0)),
                  pl.BlockSpec((w, d), lambda i: (0, 0))],
        out_specs=pl.BlockSpec((tm, d), lambda i: (i, 0)),
        out_shape=jax.ShapeDtypeStruct((n, d), BF16),
        compiler_params=_params("parallel"),
        name="branch_merge",
    )(oa, ob, gates, wa, wb)


def _out_proj_kernel(m_ref, w_ref, r_ref, g_ref, x_ref, *h_refs):
    x = jnp.dot(m_ref[...], w_ref[...], preferred_element_type=F32) + r_ref[...]
    x_ref[...] = x
    y = _rms(x, g_ref[...])
    for h_ref in h_refs:
        h_ref[...] = y.astype(h_ref.dtype)


def _out_proj(merged, w, res, g, norm_dtypes):
    n, d = res.shape
    k = merged.shape[1]
    tm = _tile(n, TOKEN_TILE // 2)
    row = lambda c: pl.BlockSpec((tm, c), lambda i: (i, 0))
    return pl.pallas_call(
        _out_proj_kernel,
        grid=(n // tm,),
        in_specs=[row(k), pl.BlockSpec((k, d), lambda i: (0, 0)), row(d),
                  pl.BlockSpec((1, d), lambda i: (0, 0))],
        out_specs=[row(d)] + [row(d) for _ in norm_dtypes],
        out_shape=[jax.ShapeDtypeStruct((n, d), F32)]
                  + [jax.ShapeDtypeStruct((n, d), dt) for dt in norm_dtypes],
        compiler_params=_params("parallel"),
        name="proj_out",
    )(merged, w, res, g.reshape(1, d).astype(F32))


def _ffn_kernel(te_ref, tv_ref, x_ref, wg_ref, wu_ref, wd_ref, *rest, has_res):
    if has_res:
        res_ref, o_ref, xb_ref, acc_ref = rest
    else:
        res_ref = None
        o_ref, xb_ref, acc_ref = rest
    i = pl.program_id(0)
    j = pl.program_id(1)
    last = pl.num_programs(1) - 1
    valid = tv_ref[i] > 0

    @pl.when(jnp.logical_and(valid, j == 0))
    def _():
        xb_ref[...] = x_ref[...].astype(BF16)
        acc_ref[...] = jnp.zeros_like(acc_ref)

    @pl.when(valid)
    def _():
        x = xb_ref[...]
        g = jnp.dot(x, wg_ref[...], preferred_element_type=F32)
        u = jnp.dot(x, wu_ref[...], preferred_element_type=F32)
        a = (g * jax.nn.sigmoid(g)) * u
        acc_ref[...] += jnp.dot(a.astype(BF16), wd_ref[...], preferred_element_type=F32)

    @pl.when(jnp.logical_and(valid, j == last))
    def _():
        out = acc_ref[...]
        if has_res:
            out = out + res_ref[...]
        o_ref[...] = out

    @pl.when(jnp.logical_and(jnp.logical_not(valid), j == last))
    def _():
        o_ref[...] = jnp.zeros_like(o_ref)


def _ffn(x, wg, wu, wd, tile_expert, tile_valid, tm, name, res=None):
    n, d = x.shape
    f = wg.shape[2]
    tf = _tile(f, FF_TILE)
    nj = f // tf

    def fj(i, j, tv):
        return jnp.where(tv[i] > 0, j, nj - 1)

    in_specs = [pl.BlockSpec((tm, d), lambda i, j, te, tv: (i, 0)),
                pl.BlockSpec((None, d, tf), lambda i, j, te, tv: (te[i], 0, fj(i, j, tv))),
                pl.BlockSpec((None, d, tf), lambda i, j, te, tv: (te[i], 0, fj(i, j, tv))),
                pl.BlockSpec((None, tf, d), lambda i, j, te, tv: (te[i], fj(i, j, tv), 0))]
    args = [x, wg, wu, wd]
    if res is not None:
        in_specs.append(pl.BlockSpec((tm, d), lambda i, j, te, tv: (i, 0)))
        args.append(res)
    return pl.pallas_call(
        functools.partial(_ffn_kernel, has_res=res is not None),
        grid_spec=pltpu.PrefetchScalarGridSpec(
            num_scalar_prefetch=2,
            grid=(n // tm, nj),
            in_specs=in_specs,
            out_specs=pl.BlockSpec((tm, d), lambda i, j, te, tv: (i, 0)),
            scratch_shapes=[pltpu.VMEM((tm, d), BF16), pltpu.VMEM((tm, d), F32)]),
        out_shape=jax.ShapeDtypeStruct((n, d), F32),
        compiler_params=_params("parallel", "arbitrary"),
        name=name,
    )(tile_expert, tile_valid, *args)


def _router_kernel(h_ref, w_ref, b_ref, idx_ref, gate_ref, cnt_ref, run_ref, *, tm):
    i = pl.program_id(0)

    @pl.when(i == 0)
    def _():
        run_ref[...] = jnp.zeros_like(run_ref)

    logits = jnp.dot(h_ref[...], w_ref[...], preferred_element_type=F32) + b_ref[...]
    lane = lax.broadcasted_iota(jnp.int32, logits.shape, 1)
    m1 = jnp.max(logits, axis=1, keepdims=True)
    i1 = jnp.min(jnp.where(logits == m1, lane, LANES), axis=1, keepdims=True)
    oh1 = lane == i1
    rest = jnp.where(oh1, -jnp.inf, logits)
    m2 = jnp.max(rest, axis=1, keepdims=True)
    i2 = jnp.min(jnp.where(rest == m2, lane, LANES), axis=1, keepdims=True)
    oh2 = lane == i2
    e = jnp.exp(m2 - m1)
    g1 = 1.0 / (1.0 + e)
    g2 = e / (1.0 + e)

    row = lax.broadcasted_iota(jnp.int32, (tm, tm), 0)
    col = lax.broadcasted_iota(jnp.int32, (tm, tm), 1)
    tri = (col < row).astype(BF16)
    picked = jnp.logical_or(oh1, oh2).astype(BF16)
    before = jnp.dot(tri, picked, preferred_element_type=F32) + run_ref[...]
    r1 = jnp.sum(jnp.where(oh1, before, 0.0), axis=1, keepdims=True).astype(jnp.int32)
    r2 = jnp.sum(jnp.where(oh2, before, 0.0), axis=1, keepdims=True).astype(jnp.int32)
    run_ref[...] += jnp.sum(picked.astype(F32), axis=0, keepdims=True)

    idx_ref[...] = jnp.where(lane == 0, i1, jnp.where(lane == 1, i2,
                             jnp.where(lane == 2, r1, jnp.where(lane == 3, r2, 0))))
    gate_ref[...] = jnp.where(lane == 0, g1, jnp.where(lane == 1, g2, 0.0))
    cnt_ref[...] = run_ref[...]


def _router(h, w_r, b_r):
    n, d = h.shape
    ne = w_r.shape[1]
    tm = _tile(n, TOKEN_TILE)
    w_pad = jnp.zeros((d, LANES), BF16).at[:, :ne].set(w_r.astype(BF16))
    b_pad = jnp.full((1, LANES), NEG_INF, F32).at[0, :ne].set(b_r.astype(F32))
    return pl.pallas_call(
        functools.partial(_router_kernel, tm=tm),
        grid=(n // tm,),
        in_specs=[pl.BlockSpec((tm, d), lambda i: (i, 0)),
                  pl.BlockSpec((d, LANES), lambda i: (0, 0)),
                  pl.BlockSpec((1, LANES), lambda i: (0, 0))],
        out_specs=[pl.BlockSpec((tm, LANES), lambda i: (i, 0)),
                   pl.BlockSpec((tm, LANES), lambda i: (i, 0)),
                   pl.BlockSpec((1, LANES), lambda i: (0, 0))],
        out_shape=[jax.ShapeDtypeStruct((n, LANES), jnp.int32),
                   jax.ShapeDtypeStruct((n, LANES), F32),
                   jax.ShapeDtypeStruct((1, LANES), F32)],
        scratch_shapes=[pltpu.VMEM((1, LANES), F32)],
        compiler_params=_params("arbitrary"),
        name="moe_router",
    )(h, w_pad, b_pad)


def _row_copy(src, src_row, dst, dst_row, sem):
    return pltpu.make_async_copy(src.at[pl.ds(src_row, 1), :], dst.at[pl.ds(dst_row, 1), :], sem)


def _dispatch_kernel(src_ref, h_ref, o_ref, buf_ref, sem, *, tt):
    def issue(r, c):
        _row_copy(h_ref, src_ref[0, 0, r], buf_ref, r, sem).start()
        return c

    lax.fori_loop(0, tt, issue, 0)

    def drain(r, c):
        _row_copy(h_ref, 0, buf_ref, 0, sem).wait()
        return c

    lax.fori_loop(0, tt, drain, 0)
    o_ref[...] = buf_ref[...].astype(o_ref.dtype)


def _dispatch(h, src):
    d = h.shape[1]
    n_slots = src.shape[0]
    tt = _tile(n_slots, ROW_DMA_TILE)
    return pl.pallas_call(
        functools.partial(_dispatch_kernel, tt=tt),
        grid=(n_slots // tt,),
        in_specs=[pl.BlockSpec((1, 1, tt), lambda i: (i, 0, 0), memory_space=pltpu.SMEM),
                  pl.BlockSpec(memory_space=pl.ANY)],
        out_specs=pl.BlockSpec((tt, d), lambda i: (i, 0)),
        out_shape=jax.ShapeDtypeStruct((n_slots, d), BF16),
        scratch_shapes=[pltpu.VMEM((tt, d), h.dtype), pltpu.SemaphoreType.DMA(())],
        compiler_params=_params("arbitrary"),
        name="moe_dispatch",
    )(src.reshape(n_slots // tt, 1, tt), h)


def _combine_kernel(pos_ref, x_ref, gate_ref, y_ref, g_ref, *rest, tt, want_x):
    if want_x:
        x_out_ref, o_ref, buf_ref, sem = rest
    else:
        o_ref, buf_ref, sem = rest

    def issue(r, c):
        for k in range(TOP_K):
            _row_copy(y_ref, pos_ref[0, 0, TOP_K * r + k], buf_ref.at[k], r, sem).start()
        return c

    lax.fori_loop(0, tt, issue, 0)

    def drain(r, c):
        for k in range(TOP_K):
            _row_copy(y_ref, 0, buf_ref.at[k], 0, sem).wait()
        return c

    lax.fori_loop(0, tt, drain, 0)
    g = gate_ref[...]
    x = x_ref[...] + (g[:, 0:1] * buf_ref[0] + g[:, 1:2] * buf_ref[1])
    if want_x:
        x_out_ref[...] = x
    o_ref[...] = _rms(x, g_ref[...]).astype(o_ref.dtype)


def _combine_norm(x, gates, y, pos, g, row0, n_rows, norm_dtype, want_x):
    n, d = x.shape
    tt = _tile(n_rows, ROW_DMA_TILE)
    assert n % tt == 0 and row0 % tt == 0
    off = row0 // tt
    pos3 = pos.reshape(n // tt, 1, TOP_K * tt)
    out = pl.BlockSpec((tt, d), lambda i: (i, 0))
    return pl.pallas_call(
        functools.partial(_combine_kernel, tt=tt, want_x=want_x),
        grid=(n_rows // tt,),
        in_specs=[pl.BlockSpec((1, 1, TOP_K * tt), lambda i: (off + i, 0, 0), memory_space=pltpu.SMEM),
                  pl.BlockSpec((tt, d), lambda i: (off + i, 0)),
                  pl.BlockSpec((tt, LANES), lambda i: (off + i, 0)),
                  pl.BlockSpec(memory_space=pl.ANY),
                  pl.BlockSpec((1, d), lambda i: (0, 0))],
        out_specs=[out] * (2 if want_x else 1),
        out_shape=([jax.ShapeDtypeStruct((n_rows, d), F32)] if want_x else [])
                  + [jax.ShapeDtypeStruct((n_rows, d), norm_dtype)],
        scratch_shapes=[pltpu.VMEM((TOP_K, tt, d), F32), pltpu.SemaphoreType.DMA(())],
        compiler_params=_params("arbitrary"),
        name="moe_combine",
    )(pos3, x, gates, y, g.reshape(1, d).astype(F32))


def _moe_route(h_bf, w_r, b_r, ne, tm):
    n = h_bf.shape[0]
    idx, gates, counts = _router(h_bf, w_r, b_r)
    counts = counts[0, :ne].astype(jnp.int32)
    padded = ((counts + tm - 1) // tm) * tm
    ends = jnp.cumsum(padded)
    offsets = ends - padded
    expert = idx[:, :TOP_K]
    pos = jnp.zeros_like(expert)
    for e in range(ne):
        pos = jnp.where(expert == e, offsets[e], pos)
    pos = pos + idx[:, TOP_K:2 * TOP_K]
    n_tiles = (n * TOP_K) // tm + ne
    tile_start = jnp.arange(n_tiles, dtype=jnp.int32) * tm
    tile_valid = (tile_start < ends[-1]).astype(jnp.int32)
    tile_expert = jnp.minimum(jnp.sum((tile_start[:, None] >= ends[None, :]).astype(jnp.int32), axis=1),
                              ne - 1)
    last_expert = jnp.max(jnp.where(counts > 0, jnp.arange(ne, dtype=jnp.int32), 0))
    tile_expert = jnp.where(tile_valid > 0, tile_expert, last_expert)
    token = jnp.broadcast_to(jnp.arange(n, dtype=jnp.int32)[:, None], pos.shape)
    src = jnp.zeros((n_tiles * tm,), jnp.int32).at[pos.reshape(-1)].set(
        token.reshape(-1), unique_indices=True)
    return pos, src, gates, tile_expert, tile_valid


def kernel(x_prompt, x_sample, cache_diff_k, cache_diff_v, cache_sb_k, cache_sb_v, rel_bias, norm_mix_g, w_in, lambda_q1, lambda_k1, lambda_q2, lambda_k2, subln_g, w_branch_a, w_branch_b, w_out, norm_ffn_g, dense_w_gate, dense_w_up, dense_w_down, router_w, router_b, moe_w_gate, moe_w_up, moe_w_down, final_norm_g):
    batch, seq, d = x_prompt.shape
    db, ts, _ = x_sample.shape
    depth, _, past, heads, w2 = cache_diff_k.shape
    d_a = w2 // 2
    d_b = cache_sb_k.shape[-1]
    wid = heads * w2
    n_p = batch * seq
    n_s = db * ts
    n_tok = n_p + n_s
    assert d % wid == 0 and w_in.shape[2] == 6 * wid + 2 * d
    assert cache_sb_k.shape[3] * d_b == wid and cache_diff_v.shape[-1] == w2

    x = jnp.concatenate([x_prompt.reshape(n_p, d), x_sample.reshape(n_s, d)], axis=0)

    t_attn = _tile(seq, ATTN_TILE)
    pos_q = past + jnp.arange(ts, dtype=jnp.int32)
    bias_tiles = _prompt_bias_tiles(rel_bias, t_attn)
    bias_c = _masked_bias(rel_bias, pos_q, jnp.arange(past, dtype=jnp.int32))
    bias_n = _masked_bias(rel_bias, pos_q, pos_q)

    cache_p = [None] * 4
    cache_s = [None] * 4
    kv_cols = (1, 2, 4, 5)
    (h,) = _rmsnorm(x, norm_mix_g[0], [BF16], "norm_mix")
    for l in range(depth):
        lam_init = 0.8 - 0.6 * math.exp(-0.3 * l)
        (q,) = _proj(h, w_in, l, lambda j: 3 * j, 2, wid,
                     [((n_tok, 2 * wid), BF16, (_tile(n_tok, TOKEN_TILE), wid), lambda j, i: (i, j), None)],
                     "proj_q")
        (gates,) = _proj(h, w_in, l, lambda j: j + 6, 2 * d // wid, wid,
                         [((n_tok, 2 * d), F32, (_tile(n_tok, TOKEN_TILE), wid), lambda j, i: (i, j), None)],
                         "proj_gate")
        kv_p, kv_s = [], []
        for c in range(4):
            cache_p[c], b16 = _proj_cache_rows(h, w_in, l, kv_cols[c], wid, 0, n_p, cache_p[c], "proj_kv_prompt")
            kv_p.append(b16)
            cache_s[c], b16 = _proj_cache_rows(h, w_in, l, kv_cols[c], wid, n_p, n_s, cache_s[c], "proj_kv_sample")
            kv_s.append(b16)

        lam_vecs = [v[l].reshape(1, d_a).astype(F32) for v in (lambda_q1, lambda_k1, lambda_q2, lambda_k2)]
        g_sub = subln_g[l].reshape(1, w2).astype(F32)
        oa, ob = _prompt_attention(q, kv_p, bias_tiles, lam_vecs, g_sub, n_tok, batch, seq, heads, d_a, d_b,
                                   t_attn, lam_init)
        caches = [c[l].reshape(db, past, wid) for c in (cache_diff_k, cache_diff_v, cache_sb_k, cache_sb_v)]
        oa, ob = _sample_attention(q, kv_s, caches, bias_c, bias_n, lam_vecs, g_sub, oa, ob, n_p, db, ts,
                                   heads, d_a, d_b, lam_init)
        merged = _merge(oa, ob, gates, w_branch_a[l].astype(BF16), w_branch_b[l].astype(BF16))

        j = l // 2
        last = l + 1 == depth
        tm = _tile(n_tok, TOKEN_TILE)
        if l % 2 == 0:
            x, h2 = _out_proj(merged, w_out[l].astype(BF16), x, norm_ffn_g[l], [BF16])
            n_tiles = n_tok // tm
            x = _ffn(h2, dense_w_gate[j:j + 1].astype(BF16), dense_w_up[j:j + 1].astype(BF16),
                     dense_w_down[j:j + 1].astype(BF16), jnp.zeros((n_tiles,), jnp.int32),
                     jnp.ones((n_tiles,), jnp.int32), tm, "dense_ffn", res=x)
            if last:
                (y_prompt,) = _rmsnorm(x, final_norm_g, [F32], "norm_final", 0, n_p)
                (y_sample,) = _rmsnorm(x, final_norm_g, [F32], "norm_final", n_p, n_s)
            else:
                (h,) = _rmsnorm(x, norm_mix_g[l + 1], [BF16], "norm_mix")
        else:
            x, h2, h2_f32 = _out_proj(merged, w_out[l].astype(BF16), x, norm_ffn_g[l], [BF16, F32])
            ne = moe_w_gate.shape[1]
            pos, src, route_g, tile_expert, tile_valid = _moe_route(h2, router_w[j], router_b[j], ne, tm)
            xs = _dispatch(h2_f32, src)
            y = _ffn(xs, moe_w_gate[j].astype(BF16), moe_w_up[j].astype(BF16), moe_w_down[j].astype(BF16),
                     tile_expert, tile_valid, tm, "moe_ffn")
            if last:
                (y_prompt,) = _combine_norm(x, route_g, y, pos, final_norm_g, 0, n_p, F32, False)
                (y_sample,) = _combine_norm(x, route_g, y, pos, final_norm_g, n_p, n_s, F32, False)
            else:
                x, h = _combine_norm(x, route_g, y, pos, norm_mix_g[l + 1], 0, n_tok, BF16, True)

    outs_p = [c.reshape(depth, batch, seq, heads, w2) for c in cache_p]
    outs_s = [c.reshape(depth, db, ts, heads, w2) for c in cache_s]
    return (y_prompt.reshape(batch, seq, d), y_sample.reshape(db, ts, d), *outs_p, *outs_s)
```

```python
import functools
import math

import jax
import jax.numpy as jnp
from jax import lax
from jax.experimental import pallas as pl
from jax.experimental.pallas import tpu as pltpu

CHUNK = 64
NUM_BUCKETS = 32
MAX_DISTANCE = 128
TOP_K = 2
EPS = 1e-6
NEG_INF = -1e30

LANES = 128
VMEM_LIMIT_BYTES = 56 * 1024 * 1024
TOKEN_TILE = 512
ATTN_TILE = 512
FF_TILE = 512
ROW_DMA_BYTES = 4 * 1024 * 1024

BF16 = jnp.bfloat16
F32 = jnp.float32


def _params(*sem):
    return pltpu.CompilerParams(dimension_semantics=sem, vmem_limit_bytes=VMEM_LIMIT_BYTES)


def _tile(n, want):
    t = min(n, want)
    while n % t:
        t //= 2
    return t


def _rms(x, g):
    return x * lax.rsqrt(jnp.mean(x * x, axis=-1, keepdims=True) + EPS) * g


def _rmsnorm_kernel(x_ref, g_ref, *o_refs):
    y = _rms(x_ref[...].astype(F32), g_ref[...])
    for o_ref in o_refs:
        o_ref[...] = y.astype(o_ref.dtype)


def _rmsnorm(x, g, out_dtypes, name, row0=0, n_rows=None):
    d = x.shape[1]
    n_rows = x.shape[0] if n_rows is None else n_rows
    tm = _tile(n_rows, TOKEN_TILE)
    assert row0 % tm == 0
    off = row0 // tm
    return pl.pallas_call(
        _rmsnorm_kernel,
        grid=(n_rows // tm,),
        in_specs=[pl.BlockSpec((tm, d), lambda i: (off + i, 0)),
                  pl.BlockSpec((1, d), lambda i: (0, 0))],
        out_specs=[pl.BlockSpec((tm, d), lambda i: (i, 0)) for _ in out_dtypes],
        out_shape=[jax.ShapeDtypeStruct((n_rows, d), dt) for dt in out_dtypes],
        compiler_params=_params("parallel"),
        name=name,
    )(x, g.reshape(1, d).astype(F32))


def _proj_kernel(x_ref, w_ref, *rest, n_out):
    o_refs, wb_ref = rest[-1 - n_out:-1], rest[-1]

    @pl.when(pl.program_id(1) == 0)
    def _():
        wb_ref[...] = w_ref[...].astype(BF16)

    acc = jnp.dot(x_ref[...], wb_ref[...], preferred_element_type=F32)
    for o_ref in o_refs:
        o_ref[...] = acc.astype(o_ref.dtype)


def _proj(x, w, layer, col_block_fn, n_blocks, tn, outs, name, row0=0, n_rows=None):
    k = x.shape[1]
    n_rows = x.shape[0] if n_rows is None else n_rows
    tm = _tile(n_rows, TOKEN_TILE)
    assert row0 % tm == 0
    off = row0 // tm
    return pl.pallas_call(
        functools.partial(_proj_kernel, n_out=len(outs)),
        grid=(n_blocks, n_rows // tm),
        in_specs=[pl.BlockSpec((tm, k), lambda j, i: (off + i, 0)),
                  pl.BlockSpec((None, k, tn), lambda j, i: (layer, 0, col_block_fn(j)))],
        out_specs=[pl.BlockSpec(o[2], o[3]) for o in outs],
        out_shape=[jax.ShapeDtypeStruct(o[0], o[1]) for o in outs],
        scratch_shapes=[pltpu.VMEM((k, tn), BF16)],
        compiler_params=_params("parallel", "arbitrary"),
        name=name,
    )(x, w)


def _cache_rows_kernel(x_ref, w_ref, *rest):
    o32_ref, o16_ref, wb_ref = rest[-3:]
    s = pl.program_id(0)

    @pl.when(jnp.logical_and(s == 0, pl.program_id(1) == 0))
    def _():
        wb_ref[...] = w_ref[...].astype(BF16)

    @pl.when(s == 0)
    def _():
        acc = jnp.dot(x_ref[...], wb_ref[...], preferred_element_type=F32)
        o32_ref[...] = acc
        o16_ref[...] = acc.astype(o16_ref.dtype)

    @pl.when(s > 0)
    def _():
        o32_ref[...] = jnp.zeros_like(o32_ref)


def _proj_cache_rows(h, w_in, layer, col, wid, row0, n_rows, prev, name):
    depth, k, _ = w_in.shape
    tm = _tile(n_rows, TOKEN_TILE)
    assert row0 % tm == 0
    off = row0 // tm
    nt = n_rows // tm
    tile = lambda s, i: jnp.where(s == 0, i, nt - 1)
    in_specs = [pl.BlockSpec((tm, k), lambda s, i: (off + tile(s, i), 0)),
                pl.BlockSpec((None, k, wid), lambda s, i: (layer, 0, col))]
    args = [h, w_in]
    if prev is not None:
        in_specs.append(pl.BlockSpec(memory_space=pl.ANY))
        args.append(prev)
    return pl.pallas_call(
        _cache_rows_kernel,
        grid=(1 if prev is not None else depth, nt),
        in_specs=in_specs,
        out_specs=[pl.BlockSpec((None, tm, wid), lambda s, i: ((layer + s) % depth, i, 0)),
                   pl.BlockSpec((tm, wid), lambda s, i: (tile(s, i), 0))],
        out_shape=[jax.ShapeDtypeStruct((depth, n_rows, wid), F32),
                   jax.ShapeDtypeStruct((n_rows, wid), BF16)],
        scratch_shapes=[pltpu.VMEM((k, wid), BF16)],
        input_output_aliases={2: 0} if prev is not None else {},
        compiler_params=_params("arbitrary", "arbitrary"),
        name=name,
    )(*args)


def _t5_bucket(rel):
    half = NUM_BUCKETS // 2
    max_exact = half // 2
    n = jnp.abs(rel)
    far = max_exact + (jnp.log(jnp.maximum(n, 1).astype(F32) / max_exact)
                       / math.log(MAX_DISTANCE / max_exact) * (half - max_exact)).astype(jnp.int32)
    far = jnp.minimum(far, half - 1)
    return jnp.where(rel > 0, half, 0) + jnp.where(n < max_exact, n, far)


def _masked_bias(rel_bias, q_pos, k_pos):
    bucket = _t5_bucket(k_pos[None, :] - q_pos[:, None])
    bias = jnp.zeros((rel_bias.shape[1],) + bucket.shape, F32)
    for b in range(NUM_BUCKETS):
        bias = jnp.where(bucket[None] == b, rel_bias[b].astype(F32)[:, None, None], bias)
    mask = (k_pos[None, :] // CHUNK) <= (q_pos[:, None] // CHUNK)
    return jnp.where(mask[None], bias, NEG_INF)


def _prompt_bias_tiles(rel_bias, t):
    assert t >= MAX_DISTANCE and t % CHUNK == 0
    pos = jnp.arange(t, dtype=jnp.int32)
    tiles = [_masked_bias(rel_bias, 2 * t + pos, (2 - d) * t + pos) for d in (2, 1, 0)]
    return jnp.stack(tiles, axis=1)


def _split_maps(q, d_a):
    lane = lax.broadcasted_iota(jnp.int32, q.shape, 1)
    zero = jnp.zeros_like(q)
    return jnp.concatenate([jnp.where(lane < d_a, q, zero), jnp.where(lane >= d_a, q, zero)], axis=0)


def _lambda(lq1, lk1, lq2, lk2, lam_init):
    return (jnp.exp(jnp.sum(lq1 * lk1, axis=-1, keepdims=True))
            - jnp.exp(jnp.sum(lq2 * lk2, axis=-1, keepdims=True)) + lam_init)


def _diff_finish(o, lam, g, t, lam_init):
    oa = o[:t] - lam * o[t:]
    return _rms(oa, g) * (1.0 - lam_init)


def _nt_dot(a, b):
    return lax.dot_general(a, b, (((1,), (1,)), ((), ())), preferred_element_type=F32)


def _diff_prompt_kernel(q_ref, k_ref, v_ref, bias_ref, lq1_ref, lk1_ref, lq2_ref, lk2_ref, g_ref,
                        o_in_ref, o_ref, *, t, d_a, lam_init):
    del o_in_ref
    qi = pl.program_id(2)
    scale = d_a ** -0.5
    q = q_ref[...]
    exact_scale = math.frexp(scale)[0] == 0.5
    if exact_scale:
        q = (q.astype(F32) * scale).astype(q.dtype)
    qq = _split_maps(q, d_a)
    dv = v_ref.shape[-1]

    def body(j, carry):
        start = pl.multiple_of(j * t, t)
        kb = k_ref[pl.ds(start, t), :]
        vb = v_ref[pl.ds(start, t), :]
        bt = bias_ref[jnp.clip(j - qi + 2, 0, 2)]
        s = _nt_dot(qq, kb)
        out = []
        for c, (m, l, acc) in enumerate(carry):
            sc = s[c * t:(c + 1) * t]
            sc = (sc if exact_scale else sc * scale) + bt
            m_new = jnp.maximum(m, jnp.max(sc, axis=1, keepdims=True))
            alpha = jnp.exp(m - m_new)
            p = jnp.exp(sc - m_new)
            l = alpha * l + jnp.sum(p, axis=1, keepdims=True)
            acc = alpha * acc + jnp.dot(p.astype(BF16), vb, preferred_element_type=F32)
            out.append((m_new, l, acc))
        return tuple(out)

    one = (jnp.full((t, 1), NEG_INF, F32), jnp.zeros((t, 1), F32), jnp.zeros((t, dv), F32))
    (_, l0, acc0), (_, l1, acc1) = lax.fori_loop(0, qi + 1, body, (one, one))
    lam = _lambda(lq1_ref[...], lk1_ref[...], lq2_ref[...], lk2_ref[...], lam_init)
    o = jnp.concatenate([acc0 / l0, acc1 / l1], axis=0)
    o_ref[...] = _diff_finish(o, lam, g_ref[...], t, lam_init).astype(o_ref.dtype)


def _diff_sample_kernel(q_ref, kc_ref, vc_ref, kn_ref, vn_ref, bc_ref, bn_ref, lq1_ref, lk1_ref,
                        lq2_ref, lk2_ref, g_ref, o_in_ref, o_ref, *, t, d_a, lam_init, hp):
    del o_in_ref
    scale = d_a ** -0.5
    w2 = 2 * d_a
    lam = _lambda(lq1_ref[...], lk1_ref[...], lq2_ref[...], lk2_ref[...], lam_init)
    for h in range(hp):
        cols = slice(h * w2, (h + 1) * w2)
        qq = _split_maps(q_ref[:, cols], d_a)
        bc = bc_ref[h]
        bn = bn_ref[h]
        s_c = _nt_dot(qq, kc_ref[:, cols].astype(BF16)) * scale + jnp.concatenate([bc, bc], axis=0)
        s_n = _nt_dot(qq, kn_ref[:, cols]) * scale + jnp.concatenate([bn, bn], axis=0)
        m = jnp.maximum(jnp.max(s_c, axis=1, keepdims=True), jnp.max(s_n, axis=1, keepdims=True))
        p_c = jnp.exp(s_c - m)
        p_n = jnp.exp(s_n - m)
        l = jnp.sum(p_c, axis=1, keepdims=True) + jnp.sum(p_n, axis=1, keepdims=True)
        acc = (jnp.dot(p_c.astype(BF16), vc_ref[:, cols].astype(BF16), preferred_element_type=F32)
               + jnp.dot(p_n.astype(BF16), vn_ref[:, cols], preferred_element_type=F32))
        o_ref[:, cols] = _diff_finish(acc / l, lam, g_ref[...], t, lam_init).astype(o_ref.dtype)


def _triangle(n):
    row = lax.broadcasted_iota(jnp.int32, (n, n), 0)
    col = lax.broadcasted_iota(jnp.int32, (n, n), 1)
    return (row > col).astype(BF16)


def _suffix_sums(x, tri):
    hi = x.astype(BF16)
    lo = (x - hi.astype(F32)).astype(BF16)
    return jnp.dot(hi, tri, preferred_element_type=F32) + jnp.dot(lo, tri, preferred_element_type=F32)


def _sb_span(q, kb, vb, r_sum, acc, scale, causal, n_parts):
    tk = kb.shape[0]
    part = tk // n_parts
    z = _nt_dot(q, kb) * scale
    sp = jnp.log(1.0 + jnp.exp(-jnp.abs(z)))
    ls = jnp.minimum(z, 0.0) - sp
    l1m = ls - z
    if causal is not None:
        l1m = jnp.where(causal, l1m, 0.0)
    tri = _triangle(part)
    between = [None] * n_parts
    for n in reversed(range(n_parts)):
        x = l1m[:, n * part:(n + 1) * part]
        between[n] = _suffix_sums(x, tri) + r_sum
        r_sum = r_sum + jnp.sum(x, axis=1, keepdims=True)
    a = jnp.exp(ls + jnp.concatenate(between, axis=1))
    if causal is not None:
        a = jnp.where(causal, a, 0.0)
    acc = acc + jnp.dot(a.astype(BF16), vb, preferred_element_type=F32)
    return r_sum, acc


def _strict_causal(tq, tk):
    row = lax.broadcasted_iota(jnp.int32, (tq, tk), 0)
    col = lax.broadcasted_iota(jnp.int32, (tq, tk), 1)
    return col < row


def _sb_parts(t):
    return 2 if t % 256 == 0 else 1


def _sb_prompt_kernel(q_ref, k_ref, v_ref, o_in_ref, o_ref, *, t, d_b):
    del o_in_ref
    qi = pl.program_id(2)
    q = q_ref[...]
    scale = d_b ** -0.5
    dv = v_ref.shape[-1]
    parts = _sb_parts(t)
    start = pl.multiple_of(qi * t, t)
    carry = _sb_span(q, k_ref[pl.ds(start, t), :], v_ref[pl.ds(start, t), :],
                     jnp.zeros((t, 1), F32), jnp.zeros((t, dv), F32), scale, _strict_causal(t, t), parts)

    def body(n, carry):
        s = pl.multiple_of((qi - 1 - n) * t, t)
        return _sb_span(q, k_ref[pl.ds(s, t), :], v_ref[pl.ds(s, t), :], carry[0], carry[1],
                        scale, None, parts)

    _, acc = lax.fori_loop(0, qi, body, carry)
    o_ref[...] = acc.astype(o_ref.dtype)


def _sb_sample_kernel(q_ref, kc_ref, vc_ref, kn_ref, vn_ref, o_in_ref, o_ref, *, t, d_b, tc, hp):
    del o_in_ref
    scale = d_b ** -0.5
    past = kc_ref.shape[0]
    heads = [slice(h * d_b, (h + 1) * d_b) for h in range(hp)]
    qs = [q_ref[:, c] for c in heads]
    causal = _strict_causal(t, t)
    carry = tuple(_sb_span(qs[h], kn_ref[:, c], vn_ref[:, c], jnp.zeros((t, 1), F32),
                           jnp.zeros((t, d_b), F32), scale, causal, 1)
                  for h, c in enumerate(heads))

    def body(n, carry):
        s = pl.multiple_of(past - (n + 1) * tc, tc)
        return tuple(_sb_span(qs[h], kc_ref[pl.ds(s, tc), c].astype(BF16),
                              vc_ref[pl.ds(s, tc), c].astype(BF16), carry[h][0], carry[h][1],
                              scale, None, max(1, tc // 256))
                     for h, c in enumerate(heads))

    carry = lax.fori_loop(0, past // tc, body, carry)
    for h, c in enumerate(heads):
        o_ref[:, c] = carry[h][1].astype(o_ref.dtype)


def _prompt_attention(q, kv, bias_tiles, lam_vecs, subln_g, n_tok, batch, seq, heads, d_a, d_b, t,
                      lam_init):
    nq = seq // t
    grid = (batch, heads, nq)
    w2 = 2 * d_a
    q_spec = lambda off: pl.BlockSpec((t, w2), lambda b, h, i: (b * nq + i, off + h))
    kv_spec = pl.BlockSpec((seq, w2), lambda b, h, i: (b, h))
    o_spec = pl.BlockSpec((t, w2), lambda b, h, i: (b * nq + i, h))
    o_shape = jax.ShapeDtypeStruct((n_tok, heads * w2), BF16)
    vec = pl.BlockSpec((1, d_a), lambda b, h, i: (0, 0))
    any_spec = pl.BlockSpec(memory_space=pl.ANY)
    blank = jnp.zeros(o_shape.shape, o_shape.dtype)
    oa = pl.pallas_call(
        functools.partial(_diff_prompt_kernel, t=t, d_a=d_a, lam_init=lam_init),
        grid=grid,
        in_specs=[q_spec(0), kv_spec, kv_spec,
                  pl.BlockSpec((None, 3, t, t), lambda b, h, i: (h, 0, 0, 0)),
                  vec, vec, vec, vec,
                  pl.BlockSpec((1, w2), lambda b, h, i: (0, 0)),
                  any_spec],
        out_specs=o_spec, out_shape=o_shape,
        input_output_aliases={9: 0},
        compiler_params=_params("parallel", "parallel", "parallel"),
        name="diff_attn_prompt",
    )(q, kv[0], kv[1], bias_tiles, *lam_vecs, subln_g, blank)
    assert d_b == w2
    ob = pl.pallas_call(
        functools.partial(_sb_prompt_kernel, t=t, d_b=d_b),
        grid=grid,
        in_specs=[q_spec(heads), kv_spec, kv_spec, any_spec],
        out_specs=o_spec, out_shape=o_shape,
        input_output_aliases={3: 0},
        compiler_params=_params("parallel", "parallel", "parallel"),
        name="sb_attn_prompt",
    )(q, kv[2], kv[3], blank)
    return oa, ob


def _sample_attention(q, kv, caches, layer, bias_c, bias_n, lam_vecs, subln_g, oa, ob, n_p, db, ts,
                      heads, d_a, d_b, lam_init):
    past = caches[0].shape[2]
    w2 = 2 * d_a
    hp = 4 if heads % 4 == 0 else 1
    wg = hp * w2
    ng = heads // hp
    assert n_p % ts == 0
    row0 = n_p // ts
    grid = (db, ng)
    q_spec = lambda off: pl.BlockSpec((ts, wg), lambda b, g: (row0 + b, off + g))
    n_spec = pl.BlockSpec((ts, wg), lambda b, g: (b, g))
    c_spec = pl.BlockSpec((None, None, past, wg), lambda b, g: (layer, b, 0, g))
    o_spec = pl.BlockSpec((ts, wg), lambda b, g: (row0 + b, g))
    any_spec = pl.BlockSpec(memory_space=pl.ANY)
    vec = pl.BlockSpec((1, d_a), lambda b, g: (0, 0))
    oa = pl.pallas_call(
        functools.partial(_diff_sample_kernel, t=ts, d_a=d_a, lam_init=lam_init, hp=hp),
        grid=grid,
        in_specs=[q_spec(0), c_spec, c_spec, n_spec, n_spec,
                  pl.BlockSpec((hp, ts, past), lambda b, g: (g, 0, 0)),
                  pl.BlockSpec((hp, ts, ts), lambda b, g: (g, 0, 0)),
                  vec, vec, vec, vec,
                  pl.BlockSpec((1, w2), lambda b, g: (0, 0)),
                  any_spec],
        out_specs=o_spec, out_shape=jax.ShapeDtypeStruct(oa.shape, oa.dtype),
        input_output_aliases={12: 0},
        compiler_params=_params("parallel", "parallel"),
        name="diff_attn_sample",
    )(q, caches[0], caches[1], kv[0], kv[1], bias_c, bias_n, *lam_vecs, subln_g, oa)
    tc = _tile(past, 1024)
    ob = pl.pallas_call(
        functools.partial(_sb_sample_kernel, t=ts, d_b=d_b, tc=tc, hp=hp),
        grid=grid,
        in_specs=[q_spec(ng), c_spec, c_spec, n_spec, n_spec, any_spec],
        out_specs=o_spec, out_shape=jax.ShapeDtypeStruct(ob.shape, ob.dtype),
        input_output_aliases={5: 0},
        compiler_params=_params("parallel", "parallel"),
        name="sb_attn_sample",
    )(q, caches[2], caches[3], kv[2], kv[3], ob)
    return oa, ob


def _merge_kernel(oa_ref, ob_ref, g_ref, wa_ref, wb_ref, o_ref, *, d):
    a = jnp.dot(oa_ref[...], wa_ref[...], preferred_element_type=F32)
    b = jnp.dot(ob_ref[...], wb_ref[...], preferred_element_type=F32)
    ga = jax.nn.sigmoid(g_ref[:, :d])
    gb = jax.nn.sigmoid(g_ref[:, d:])
    o_ref[...] = (ga * a + gb * b).astype(o_ref.dtype)


def _merge(oa, ob, gates, wa, wb):
    n, w = oa.shape
    d = wa.shape[1]
    tm = _tile(n, TOKEN_TILE // 2)
    return pl.pallas_call(
        functools.partial(_merge_kernel, d=d),
        grid=(n // tm,),
        in_specs=[pl.BlockSpec((tm, w), lambda i: (i, 0)),
                  pl.BlockSpec((tm, w), lambda i: (i, 0)),
                  pl.BlockSpec((tm, 2 * d), lambda i: (i, 0)),
                  pl.BlockSpec((w, d), lambda i: (0, 0)),
                  pl.BlockSpec((w, d), lambda i: (0, 0))],
        out_specs=pl.BlockSpec((tm, d), lambda i: (i, 0)),
        out_shape=jax.ShapeDtypeStruct((n, d), BF16),
        compiler_params=_params("parallel"),
        name="branch_merge",
    )(oa, ob, gates, wa, wb)


def _out_proj_kernel(m_ref, w_ref, r_ref, g_ref, x_ref, *h_refs):
    x = jnp.dot(m_ref[...], w_ref[...], preferred_element_type=F32) + r_ref[...]
    x_ref[...] = x
    y = _rms(x, g_ref[...])
    for h_ref in h_refs:
        h_ref[...] = y.astype(h_ref.dtype)


def _out_proj(merged, w, res, g, norm_dtypes):
    n, d = res.shape
    k = merged.shape[1]
    tm = _tile(n, TOKEN_TILE // 2)
    row = lambda c: pl.BlockSpec((tm, c), lambda i: (i, 0))
    return pl.pallas_call(
        _out_proj_kernel,
        grid=(n // tm,),
        in_specs=[row(k), pl.BlockSpec((k, d), lambda i: (0, 0)), row(d),
                  pl.BlockSpec((1, d), lambda i: (0, 0))],
        out_specs=[row(d)] + [row(d) for _ in norm_dtypes],
        out_shape=[jax.ShapeDtypeStruct((n, d), F32)]
                  + [jax.ShapeDtypeStruct((n, d), dt) for dt in norm_dtypes],
        compiler_params=_params("parallel"),
        name="proj_out",
    )(merged, w, res, g.reshape(1, d).astype(F32))


def _ffn_kernel(te_ref, tv_ref, x_ref, wg_ref, wu_ref, wd_ref, *rest, has_res):
    if has_res:
        res_ref, o_ref, xb_ref, acc_ref = rest
    else:
        res_ref = None
        o_ref, xb_ref, acc_ref = rest
    i = pl.program_id(0)
    j = pl.program_id(1)
    last = pl.num_programs(1) - 1
    valid = tv_ref[i] > 0

    @pl.when(jnp.logical_and(valid, j == 0))
    def _():
        xb_ref[...] = x_ref[...].astype(BF16)
        acc_ref[...] = jnp.zeros_like(acc_ref)

    @pl.when(valid)
    def _():
        x = xb_ref[...]
        g = jnp.dot(x, wg_ref[...], preferred_element_type=F32)
        u = jnp.dot(x, wu_ref[...], preferred_element_type=F32)
        a = (g * jax.nn.sigmoid(g)) * u
        acc_ref[...] += jnp.dot(a.astype(BF16), wd_ref[...], preferred_element_type=F32)

    @pl.when(jnp.logical_and(valid, j == last))
    def _():
        out = acc_ref[...]
        if has_res:
            out = out + res_ref[...]
        o_ref[...] = out

    @pl.when(jnp.logical_and(jnp.logical_not(valid), j == last))
    def _():
        o_ref[...] = jnp.zeros_like(o_ref)


def _ffn(x, wg, wu, wd, tile_expert, tile_valid, tm, name, res=None):
    n, d = x.shape
    f = wg.shape[2]
    tf = _tile(f, FF_TILE)
    nj = f // tf

    def fj(i, j, tv):
        return jnp.where(tv[i] > 0, j, nj - 1)

    in_specs = [pl.BlockSpec((tm, d), lambda i, j, te, tv: (i, 0)),
                pl.BlockSpec((None, d, tf), lambda i, j, te, tv: (te[i], 0, fj(i, j, tv))),
                pl.BlockSpec((None, d, tf), lambda i, j, te, tv: (te[i], 0, fj(i, j, tv))),
                pl.BlockSpec((None, tf, d), lambda i, j, te, tv: (te[i], fj(i, j, tv), 0))]
    args = [x, wg, wu, wd]
    if res is not None:
        in_specs.append(pl.BlockSpec((tm, d), lambda i, j, te, tv: (i, 0)))
        args.append(res)
    return pl.pallas_call(
        functools.partial(_ffn_kernel, has_res=res is not None),
        grid_spec=pltpu.PrefetchScalarGridSpec(
            num_scalar_prefetch=2,
            grid=(n // tm, nj),
            in_specs=in_specs,
            out_specs=pl.BlockSpec((tm, d), lambda i, j, te, tv: (i, 0)),
            scratch_shapes=[pltpu.VMEM((tm, d), BF16), pltpu.VMEM((tm, d), F32)]),
        out_shape=jax.ShapeDtypeStruct((n, d), F32),
        compiler_params=_params("parallel", "arbitrary"),
        name=name,
    )(tile_expert, tile_valid, *args)


def _router_kernel(h_ref, w_ref, b_ref, idx_ref, gate_ref, cnt_ref, run_ref, *, tm):
    i = pl.program_id(0)

    @pl.when(i == 0)
    def _():
        run_ref[...] = jnp.zeros_like(run_ref)

    logits = jnp.dot(h_ref[...], w_ref[...], preferred_element_type=F32) + b_ref[...]
    lane = lax.broadcasted_iota(jnp.int32, logits.shape, 1)
    m1 = jnp.max(logits, axis=1, keepdims=True)
    i1 = jnp.min(jnp.where(logits == m1, lane, LANES), axis=1, keepdims=True)
    oh1 = lane == i1
    rest = jnp.where(oh1, -jnp.inf, logits)
    m2 = jnp.max(rest, axis=1, keepdims=True)
    i2 = jnp.min(jnp.where(rest == m2, lane, LANES), axis=1, keepdims=True)
    oh2 = lane == i2
    e = jnp.exp(m2 - m1)
    g1 = 1.0 / (1.0 + e)
    g2 = e / (1.0 + e)

    row = lax.broadcasted_iota(jnp.int32, (tm, tm), 0)
    col = lax.broadcasted_iota(jnp.int32, (tm, tm), 1)
    tri = (col < row).astype(BF16)
    picked = jnp.logical_or(oh1, oh2).astype(BF16)
    before = jnp.dot(tri, picked, preferred_element_type=F32) + run_ref[...]
    r1 = jnp.sum(jnp.where(oh1, before, 0.0), axis=1, keepdims=True).astype(jnp.int32)
    r2 = jnp.sum(jnp.where(oh2, before, 0.0), axis=1, keepdims=True).astype(jnp.int32)
    run_ref[...] += jnp.sum(picked.astype(F32), axis=0, keepdims=True)

    idx_ref[...] = jnp.where(lane == 0, i1, jnp.where(lane == 1, i2,
                             jnp.where(lane == 2, r1, jnp.where(lane == 3, r2, 0))))
    gate_ref[...] = jnp.where(lane == 0, g1, jnp.where(lane == 1, g2, 0.0))
    cnt_ref[...] = run_ref[...]


def _router(h, w_r, b_r):
    n, d = h.shape
    ne = w_r.shape[1]
    tm = _tile(n, TOKEN_TILE)
    w_pad = jnp.zeros((d, LANES), BF16).at[:, :ne].set(w_r.astype(BF16))
    b_pad = jnp.full((1, LANES), NEG_INF, F32).at[0, :ne].set(b_r.astype(F32))
    return pl.pallas_call(
        functools.partial(_router_kernel, tm=tm),
        grid=(n // tm,),
        in_specs=[pl.BlockSpec((tm, d), lambda i: (i, 0)),
                  pl.BlockSpec((d, LANES), lambda i: (0, 0)),
                  pl.BlockSpec((1, LANES), lambda i: (0, 0))],
        out_specs=[pl.BlockSpec((tm, LANES), lambda i: (i, 0)),
                   pl.BlockSpec((tm, LANES), lambda i: (i, 0)),
                   pl.BlockSpec((1, LANES), lambda i: (0, 0))],
        out_shape=[jax.ShapeDtypeStruct((n, LANES), jnp.int32),
                   jax.ShapeDtypeStruct((n, LANES), F32),
                   jax.ShapeDtypeStruct((1, LANES), F32)],
        scratch_shapes=[pltpu.VMEM((1, LANES), F32)],
        compiler_params=_params("arbitrary"),
        name="moe_router",
    )(h, w_pad, b_pad)


def _row_copy(src, src_row, dst, dst_row, sem):
    return pltpu.make_async_copy(src.at[pl.ds(src_row, 1), :], dst.at[pl.ds(dst_row, 1), :], sem)


def _dispatch_kernel(src_ref, h_ref, o_ref, buf_ref, sem, *, tt):
    def issue(r, c):
        _row_copy(h_ref, src_ref[0, 0, r], buf_ref, r, sem).start()
        return c

    lax.fori_loop(0, tt, issue, 0)

    def drain(r, c):
        _row_copy(h_ref, 0, buf_ref, 0, sem).wait()
        return c

    lax.fori_loop(0, tt, drain, 0)
    o_ref[...] = buf_ref[...].astype(o_ref.dtype)


def _dispatch(h, src):
    d = h.shape[1]
    n_slots = src.shape[0]
    tt = _tile(n_slots, ROW_DMA_BYTES // (d * h.dtype.itemsize))
    return pl.pallas_call(
        functools.partial(_dispatch_kernel, tt=tt),
        grid=(n_slots // tt,),
        in_specs=[pl.BlockSpec((1, 1, tt), lambda i: (i, 0, 0), memory_space=pltpu.SMEM),
                  pl.BlockSpec(memory_space=pl.ANY)],
        out_specs=pl.BlockSpec((tt, d), lambda i: (i, 0)),
        out_shape=jax.ShapeDtypeStruct((n_slots, d), BF16),
        scratch_shapes=[pltpu.VMEM((tt, d), h.dtype), pltpu.SemaphoreType.DMA(())],
        compiler_params=_params("arbitrary"),
        name="moe_dispatch",
    )(src.reshape(n_slots // tt, 1, tt), h)


def _combine_kernel(pos_ref, x_ref, gate_ref, y_ref, g_ref, *rest, tt, want_x):
    if want_x:
        x_out_ref, o_ref, buf_ref, sem = rest
    else:
        o_ref, buf_ref, sem = rest

    def issue(r, c):
        for k in range(TOP_K):
            _row_copy(y_ref, pos_ref[0, 0, TOP_K * r + k], buf_ref.at[k], r, sem).start()
        return c

    lax.fori_loop(0, tt, issue, 0)

    def drain(r, c):
        for k in range(TOP_K):
            _row_copy(y_ref, 0, buf_ref.at[k], 0, sem).wait()
        return c

    lax.fori_loop(0, tt, drain, 0)
    g = gate_ref[...]
    x = x_ref[...] + (g[:, 0:1] * buf_ref[0] + g[:, 1:2] * buf_ref[1])
    if want_x:
        x_out_ref[...] = x
    o_ref[...] = _rms(x, g_ref[...]).astype(o_ref.dtype)


def _combine_norm(x, gates, y, pos, g, row0, n_rows, norm_dtype, want_x):
    n, d = x.shape
    tt = _tile(n_rows, ROW_DMA_BYTES // (TOP_K * d * y.dtype.itemsize))
    assert n % tt == 0 and row0 % tt == 0
    off = row0 // tt
    pos3 = pos.reshape(n // tt, 1, TOP_K * tt)
    out = pl.BlockSpec((tt, d), lambda i: (i, 0))
    return pl.pallas_call(
        functools.partial(_combine_kernel, tt=tt, want_x=want_x),
        grid=(n_rows // tt,),
        in_specs=[pl.BlockSpec((1, 1, TOP_K * tt), lambda i: (off + i, 0, 0), memory_space=pltpu.SMEM),
                  pl.BlockSpec((tt, d), lambda i: (off + i, 0)),
                  pl.BlockSpec((tt, LANES), lambda i: (off + i, 0)),
                  pl.BlockSpec(memory_space=pl.ANY),
                  pl.BlockSpec((1, d), lambda i: (0, 0))],
        out_specs=[out] * (2 if want_x else 1),
        out_shape=([jax.ShapeDtypeStruct((n_rows, d), F32)] if want_x else [])
                  + [jax.ShapeDtypeStruct((n_rows, d), norm_dtype)],
        scratch_shapes=[pltpu.VMEM((TOP_K, tt, d), F32), pltpu.SemaphoreType.DMA(())],
        compiler_params=_params("arbitrary"),
        name="moe_combine",
    )(pos3, x, gates, y, g.reshape(1, d).astype(F32))


def _moe_route(h_bf, w_r, b_r, ne, tm):
    n = h_bf.shape[0]
    idx, gates, counts = _router(h_bf, w_r, b_r)
    counts = counts[0, :ne].astype(jnp.int32)
    padded = ((counts + tm - 1) // tm) * tm
    ends = jnp.cumsum(padded)
    offsets = ends - padded
    expert = idx[:, :TOP_K]
    pos = jnp.zeros_like(expert)
    for e in range(ne):
        pos = jnp.where(expert == e, offsets[e], pos)
    pos = pos + idx[:, TOP_K:2 * TOP_K]
    n_tiles = (n * TOP_K) // tm + ne
    tile_start = jnp.arange(n_tiles, dtype=jnp.int32) * tm
    tile_valid = (tile_start < ends[-1]).astype(jnp.int32)
    tile_expert = jnp.minimum(jnp.sum((tile_start[:, None] >= ends[None, :]).astype(jnp.int32), axis=1),
                              ne - 1)
    last_expert = jnp.max(jnp.where(counts > 0, jnp.arange(ne, dtype=jnp.int32), 0))
    tile_expert = jnp.where(tile_valid > 0, tile_expert, last_expert)
    token = jnp.broadcast_to(jnp.arange(n, dtype=jnp.int32)[:, None], pos.shape)
    src = jnp.zeros((n_tiles * tm,), jnp.int32).at[pos.reshape(-1)].set(
        token.reshape(-1), unique_indices=True)
    return pos, src, gates, tile_expert, tile_valid


def kernel(x_prompt, x_sample, cache_diff_k, cache_diff_v, cache_sb_k, cache_sb_v, rel_bias, norm_mix_g, w_in, lambda_q1, lambda_k1, lambda_q2, lambda_k2, subln_g, w_branch_a, w_branch_b, w_out, norm_ffn_g, dense_w_gate, dense_w_up, dense_w_down, router_w, router_b, moe_w_gate, moe_w_up, moe_w_down, final_norm_g):
    batch, seq, d = x_prompt.shape
    db, ts, _ = x_sample.shape
    depth, _, past, heads, w2 = cache_diff_k.shape
    d_a = w2 // 2
    d_b = cache_sb_k.shape[-1]
    wid = heads * w2
    n_p = batch * seq
    n_s = db * ts
    n_tok = n_p + n_s
    assert d % wid == 0 and w_in.shape[2] == 6 * wid + 2 * d
    assert cache_sb_k.shape[3] * d_b == wid and cache_diff_v.shape[-1] == w2

    x = jnp.concatenate([x_prompt.reshape(n_p, d), x_sample.reshape(n_s, d)], axis=0)

    t_attn = _tile(seq, ATTN_TILE)
    pos_q = past + jnp.arange(ts, dtype=jnp.int32)
    bias_tiles = _prompt_bias_tiles(rel_bias, t_attn)
    bias_c = _masked_bias(rel_bias, pos_q, jnp.arange(past, dtype=jnp.int32))
    bias_n = _masked_bias(rel_bias, pos_q, pos_q)

    cache_p = [None] * 4
    cache_s = [None] * 4
    kv_cols = (1, 2, 4, 5)
    (h,) = _rmsnorm(x, norm_mix_g[0], [BF16], "norm_mix")
    for l in range(depth):
        lam_init = 0.8 - 0.6 * math.exp(-0.3 * l)
        (q,) = _proj(h, w_in, l, lambda j: 3 * j, 2, wid,
                     [((n_tok, 2 * wid), BF16, (_tile(n_tok, TOKEN_TILE), wid), lambda j, i: (i, j))],
                     "proj_q")
        (gates,) = _proj(h, w_in, l, lambda j: j + 6, 2 * d // wid, wid,
                         [((n_tok, 2 * d), F32, (_tile(n_tok, TOKEN_TILE), wid), lambda j, i: (i, j))],
                         "proj_gate")
        kv_p, kv_s = [], []
        for c in range(4):
            cache_p[c], b16 = _proj_cache_rows(h, w_in, l, kv_cols[c], wid, 0, n_p, cache_p[c], "proj_kv_prompt")
            kv_p.append(b16)
            cache_s[c], b16 = _proj_cache_rows(h, w_in, l, kv_cols[c], wid, n_p, n_s, cache_s[c], "proj_kv_sample")
            kv_s.append(b16)

        lam_vecs = [v[l].reshape(1, d_a).astype(F32) for v in (lambda_q1, lambda_k1, lambda_q2, lambda_k2)]
        g_sub = subln_g[l].reshape(1, w2).astype(F32)
        oa, ob = _prompt_attention(q, kv_p, bias_tiles, lam_vecs, g_sub, n_tok, batch, seq, heads, d_a, d_b,
                                   t_attn, lam_init)
        caches = [c.reshape(depth, db, past, wid) for c in (cache_diff_k, cache_diff_v, cache_sb_k, cache_sb_v)]
        oa, ob = _sample_attention(q, kv_s, caches, l, bias_c, bias_n, lam_vecs, g_sub, oa, ob, n_p, db, ts,
                                   heads, d_a, d_b, lam_init)
        merged = _merge(oa, ob, gates, w_branch_a[l].astype(BF16), w_branch_b[l].astype(BF16))

        j = l // 2
        last = l + 1 == depth
        tm = _tile(n_tok, TOKEN_TILE)
        if l % 2 == 0:
            x, h2 = _out_proj(merged, w_out[l].astype(BF16), x, norm_ffn_g[l], [BF16])
            n_tiles = n_tok // tm
            x = _ffn(h2, dense_w_gate[j:j + 1].astype(BF16), dense_w_up[j:j + 1].astype(BF16),
                     dense_w_down[j:j + 1].astype(BF16), jnp.zeros((n_tiles,), jnp.int32),
                     jnp.ones((n_tiles,), jnp.int32), tm, "dense_ffn", res=x)
            if last:
                (y_prompt,) = _rmsnorm(x, final_norm_g, [F32], "norm_final", 0, n_p)
                (y_sample,) = _rmsnorm(x, final_norm_g, [F32], "norm_final", n_p, n_s)
            else:
                (h,) = _rmsnorm(x, norm_mix_g[l + 1], [BF16], "norm_mix")
        else:
            x, h2, h2_f32 = _out_proj(merged, w_out[l].astype(BF16), x, norm_ffn_g[l], [BF16, F32])
            ne = moe_w_gate.shape[1]
            pos, src, route_g, tile_expert, tile_valid = _moe_route(h2, router_w[j], router_b[j], ne, tm)
            xs = _dispatch(h2_f32, src)
            y = _ffn(xs, moe_w_gate[j].astype(BF16), moe_w_up[j].astype(BF16), moe_w_down[j].astype(BF16),
                     tile_expert, tile_valid, tm, "moe_ffn")
            if last:
                (y_prompt,) = _combine_norm(x, route_g, y, pos, final_norm_g, 0, n_p, F32, False)
                (y_sample,) = _combine_norm(x, route_g, y, pos, final_norm_g, n_p, n_s, F32, False)
            else:
                x, h = _combine_norm(x, route_g, y, pos, norm_mix_g[l + 1], 0, n_tok, BF16, True)

    outs_p = [c.reshape(depth, batch, seq, heads, w2) for c in cache_p]
    outs_s = [c.reshape(depth, db, ts, heads, w2) for c in cache_s]
    return (y_prompt.reshape(batch, seq, d), y_sample.reshape(db, ts, d), *outs_p, *outs_s)
```

```python
import functools
import math

import jax
import jax.numpy as jnp
from jax import lax
from jax.experimental import pallas as pl
from jax.experimental.pallas import tpu as pltpu

CHUNK = 64
NUM_BUCKETS = 32
MAX_DISTANCE = 128
TOP_K = 2
EPS = 1e-6
NEG_INF = -1e30

LANES = 128
VMEM_LIMIT_BYTES = 56 * 1024 * 1024
TOKEN_TILE = 512
ATTN_TILE = 512
FF_TILE = 512
ROW_DMA_BYTES = 4 * 1024 * 1024

BF16 = jnp.bfloat16
F32 = jnp.float32


def _params(*sem):
    return pltpu.CompilerParams(dimension_semantics=sem, vmem_limit_bytes=VMEM_LIMIT_BYTES)


def _tile(n, want):
    t = min(n, want)
    while n % t:
        t //= 2
    return t


def _rms(x, g):
    return x * lax.rsqrt(jnp.mean(x * x, axis=-1, keepdims=True) + EPS) * g


def _rmsnorm_kernel(x_ref, g_ref, *o_refs):
    y = _rms(x_ref[...].astype(F32), g_ref[...])
    for o_ref in o_refs:
        o_ref[...] = y.astype(o_ref.dtype)


def _rmsnorm(x, g, out_dtypes, name, row0=0, n_rows=None):
    d = x.shape[1]
    n_rows = x.shape[0] if n_rows is None else n_rows
    tm = _tile(n_rows, TOKEN_TILE)
    assert row0 % tm == 0
    off = row0 // tm
    return pl.pallas_call(
        _rmsnorm_kernel,
        grid=(n_rows // tm,),
        in_specs=[pl.BlockSpec((tm, d), lambda i: (off + i, 0)),
                  pl.BlockSpec((1, d), lambda i: (0, 0))],
        out_specs=[pl.BlockSpec((tm, d), lambda i: (i, 0)) for _ in out_dtypes],
        out_shape=[jax.ShapeDtypeStruct((n_rows, d), dt) for dt in out_dtypes],
        compiler_params=_params("parallel"),
        name=name,
    )(x, g.reshape(1, d).astype(F32))


def _join_norm_kernel(xp_ref, xs_ref, g_ref, x_ref, h_ref, *, np_tiles):
    x = jnp.where(pl.program_id(0) < np_tiles, xp_ref[...], xs_ref[...])
    x_ref[...] = x
    h_ref[...] = _rms(x, g_ref[...]).astype(h_ref.dtype)


def _join_norm(xp, xs, g):
    (n_p, d), n_s = xp.shape, xs.shape[0]
    tm = _tile(n_s, TOKEN_TILE)
    assert n_p % tm == 0
    np_tiles = n_p // tm
    out = pl.BlockSpec((tm, d), lambda i: (i, 0))
    return pl.pallas_call(
        functools.partial(_join_norm_kernel, np_tiles=np_tiles),
        grid=((n_p + n_s) // tm,),
        in_specs=[pl.BlockSpec((tm, d), lambda i: (jnp.minimum(i, np_tiles - 1), 0)),
                  pl.BlockSpec((tm, d), lambda i: (jnp.maximum(i - np_tiles, 0), 0)),
                  pl.BlockSpec((1, d), lambda i: (0, 0))],
        out_specs=[out, out],
        out_shape=[jax.ShapeDtypeStruct((n_p + n_s, d), F32), jax.ShapeDtypeStruct((n_p + n_s, d), BF16)],
        compiler_params=_params("parallel"),
        name="join_norm",
    )(xp, xs, g.reshape(1, d).astype(F32))


def _proj_kernel(x_ref, w_ref, *rest, n_out):
    o_refs, wb_ref = rest[-1 - n_out:-1], rest[-1]

    @pl.when(pl.program_id(1) == 0)
    def _():
        wb_ref[...] = w_ref[...].astype(BF16)

    acc = jnp.dot(x_ref[...], wb_ref[...], preferred_element_type=F32)
    for o_ref in o_refs:
        o_ref[...] = acc.astype(o_ref.dtype)


def _proj(x, w, layer, col_block_fn, n_blocks, tn, outs, name, row0=0, n_rows=None):
    k = x.shape[1]
    n_rows = x.shape[0] if n_rows is None else n_rows
    tm = _tile(n_rows, TOKEN_TILE)
    assert row0 % tm == 0
    off = row0 // tm
    return pl.pallas_call(
        functools.partial(_proj_kernel, n_out=len(outs)),
        grid=(n_blocks, n_rows // tm),
        in_specs=[pl.BlockSpec((tm, k), lambda j, i: (off + i, 0)),
                  pl.BlockSpec((None, k, tn), lambda j, i: (layer, 0, col_block_fn(j)))],
        out_specs=[pl.BlockSpec(o[2], o[3]) for o in outs],
        out_shape=[jax.ShapeDtypeStruct(o[0], o[1]) for o in outs],
        scratch_shapes=[pltpu.VMEM((k, tn), BF16)],
        compiler_params=_params("parallel", "arbitrary"),
        name=name,
    )(x, w)


def _cache_rows_kernel(x_ref, w_ref, *rest):
    o32_ref, o16_ref, wb_ref = rest[-3:]
    s = pl.program_id(0)

    @pl.when(jnp.logical_and(s == 0, pl.program_id(1) == 0))
    def _():
        wb_ref[...] = w_ref[...].astype(BF16)

    @pl.when(s == 0)
    def _():
        acc = jnp.dot(x_ref[...], wb_ref[...], preferred_element_type=F32)
        o32_ref[...] = acc
        o16_ref[...] = acc.astype(o16_ref.dtype)

    @pl.when(s > 0)
    def _():
        o32_ref[...] = jnp.zeros_like(o32_ref)


def _proj_cache_rows(h, w_in, layer, col, wid, row0, n_rows, prev, name):
    depth, k, _ = w_in.shape
    tm = _tile(n_rows, TOKEN_TILE)
    assert row0 % tm == 0
    off = row0 // tm
    nt = n_rows // tm
    tile = lambda s, i: jnp.where(s == 0, i, nt - 1)
    in_specs = [pl.BlockSpec((tm, k), lambda s, i: (off + tile(s, i), 0)),
                pl.BlockSpec((None, k, wid), lambda s, i: (layer, 0, col))]
    args = [h, w_in]
    if prev is not None:
        in_specs.append(pl.BlockSpec(memory_space=pl.ANY))
        args.append(prev)
    return pl.pallas_call(
        _cache_rows_kernel,
        grid=(1 if prev is not None else depth, nt),
        in_specs=in_specs,
        out_specs=[pl.BlockSpec((None, tm, wid), lambda s, i: ((layer + s) % depth, i, 0)),
                   pl.BlockSpec((tm, wid), lambda s, i: (tile(s, i), 0))],
        out_shape=[jax.ShapeDtypeStruct((depth, n_rows, wid), F32),
                   jax.ShapeDtypeStruct((n_rows, wid), BF16)],
        scratch_shapes=[pltpu.VMEM((k, wid), BF16)],
        input_output_aliases={2: 0} if prev is not None else {},
        compiler_params=_params("arbitrary", "arbitrary"),
        name=name,
    )(*args)


def _t5_bucket(rel):
    half = NUM_BUCKETS // 2
    max_exact = half // 2
    n = jnp.abs(rel)
    far = max_exact + (jnp.log(jnp.maximum(n, 1).astype(F32) / max_exact)
                       / math.log(MAX_DISTANCE / max_exact) * (half - max_exact)).astype(jnp.int32)
    far = jnp.minimum(far, half - 1)
    return jnp.where(rel > 0, half, 0) + jnp.where(n < max_exact, n, far)


def _masked_bias(rel_bias, q_pos, k_pos):
    bucket = _t5_bucket(k_pos[None, :] - q_pos[:, None])
    bias = jnp.zeros((rel_bias.shape[1],) + bucket.shape, F32)
    for b in range(NUM_BUCKETS):
        bias = jnp.where(bucket[None] == b, rel_bias[b].astype(F32)[:, None, None], bias)
    mask = (k_pos[None, :] // CHUNK) <= (q_pos[:, None] // CHUNK)
    return jnp.where(mask[None], bias, NEG_INF)


def _prompt_bias_tiles(rel_bias, t):
    assert t >= MAX_DISTANCE and t % CHUNK == 0
    pos = jnp.arange(t, dtype=jnp.int32)
    tiles = [_masked_bias(rel_bias, 2 * t + pos, (2 - d) * t + pos) for d in (2, 1, 0)]
    return jnp.stack(tiles, axis=1)


def _split_maps(q, d_a):
    lane = lax.broadcasted_iota(jnp.int32, q.shape, 1)
    zero = jnp.zeros_like(q)
    return jnp.concatenate([jnp.where(lane < d_a, q, zero), jnp.where(lane >= d_a, q, zero)], axis=0)


def _lambda(lq1, lk1, lq2, lk2, lam_init):
    return (jnp.exp(jnp.sum(lq1 * lk1, axis=-1, keepdims=True))
            - jnp.exp(jnp.sum(lq2 * lk2, axis=-1, keepdims=True)) + lam_init)


def _diff_finish(o, lam, g, t, lam_init):
    oa = o[:t] - lam * o[t:]
    return _rms(oa, g) * (1.0 - lam_init)


def _nt_dot(a, b):
    return lax.dot_general(a, b, (((1,), (1,)), ((), ())), preferred_element_type=F32)


def _diff_prompt_kernel(q_ref, k_ref, v_ref, bias_ref, lq1_ref, lk1_ref, lq2_ref, lk2_ref, g_ref,
                        o_in_ref, o_ref, *, t, d_a, lam_init):
    del o_in_ref
    qi = pl.program_id(2)
    scale = d_a ** -0.5
    q = q_ref[...]
    exact_scale = math.frexp(scale)[0] == 0.5
    if exact_scale:
        q = (q.astype(F32) * scale).astype(q.dtype)
    qq = _split_maps(q, d_a)
    dv = v_ref.shape[-1]
    half = t // 2
    assert half % CHUNK == 0

    def update(sc, bias, state, vb):
        m, l, acc = state
        sc = (sc if exact_scale else sc * scale) + bias
        m_new = jnp.maximum(m, jnp.max(sc, axis=1, keepdims=True))
        alpha = jnp.exp(m - m_new)
        p = jnp.exp(sc - m_new)
        l = alpha * l + jnp.sum(p, axis=1, keepdims=True)
        acc = alpha * acc + jnp.dot(p.astype(BF16), vb, preferred_element_type=F32)
        return m_new, l, acc

    def body(j, carry):
        start = pl.multiple_of(j * t, t)
        vb = v_ref[pl.ds(start, t), :]
        bt = bias_ref[jnp.clip(j - qi + 2, 0, 1)]
        s = _nt_dot(qq, k_ref[pl.ds(start, t), :])
        return tuple(update(s[c * t:(c + 1) * t], bt, carry[c], vb) for c in range(2))

    one = (jnp.full((t, 1), NEG_INF, F32), jnp.zeros((t, 1), F32), jnp.zeros((t, dv), F32))
    carry = lax.fori_loop(0, qi, body, (one, one))

    start = pl.multiple_of(qi * t, t)
    bt = bias_ref[2]
    rows = lambda x, c, r: x[c * t + r * half:c * t + (r + 1) * half]
    q_top = jnp.concatenate([rows(qq, 0, 0), rows(qq, 1, 0)], axis=0)
    q_bot = jnp.concatenate([rows(qq, 0, 1), rows(qq, 1, 1)], axis=0)
    s_top = _nt_dot(q_top, k_ref[pl.ds(start, half), :])
    s_bot = _nt_dot(q_bot, k_ref[pl.ds(start, t), :])
    outs = []
    for c in range(2):
        state = [tuple(x[r * half:(r + 1) * half] for x in carry[c]) for r in range(2)]
        top = update(s_top[c * half:(c + 1) * half], bt[:half, :half], state[0],
                     v_ref[pl.ds(start, half), :])
        bot = update(s_bot[c * half:(c + 1) * half], bt[half:, :], state[1], v_ref[pl.ds(start, t), :])
        outs.append(jnp.concatenate([top[2] / top[1], bot[2] / bot[1]], axis=0))
    lam = _lambda(lq1_ref[...], lk1_ref[...], lq2_ref[...], lk2_ref[...], lam_init)
    o = jnp.concatenate(outs, axis=0)
    o_ref[...] = _diff_finish(o, lam, g_ref[...], t, lam_init).astype(o_ref.dtype)


def _diff_sample_kernel(q_ref, kc_ref, vc_ref, kn_ref, vn_ref, bc_ref, bn_ref, lq1_ref, lk1_ref,
                        lq2_ref, lk2_ref, g_ref, o_in_ref, o_ref, *, t, d_a, lam_init, hp):
    del o_in_ref
    scale = d_a ** -0.5
    w2 = 2 * d_a
    lam = _lambda(lq1_ref[...], lk1_ref[...], lq2_ref[...], lk2_ref[...], lam_init)
    for h in range(hp):
        cols = slice(h * w2, (h + 1) * w2)
        qq = _split_maps(q_ref[:, cols], d_a)
        bc = bc_ref[h]
        bn = bn_ref[h]
        s_c = _nt_dot(qq, kc_ref[:, cols].astype(BF16)) * scale + jnp.concatenate([bc, bc], axis=0)
        s_n = _nt_dot(qq, kn_ref[:, cols]) * scale + jnp.concatenate([bn, bn], axis=0)
        m = jnp.maximum(jnp.max(s_c, axis=1, keepdims=True), jnp.max(s_n, axis=1, keepdims=True))
        p_c = jnp.exp(s_c - m)
        p_n = jnp.exp(s_n - m)
        l = jnp.sum(p_c, axis=1, keepdims=True) + jnp.sum(p_n, axis=1, keepdims=True)
        acc = (jnp.dot(p_c.astype(BF16), vc_ref[:, cols].astype(BF16), preferred_element_type=F32)
               + jnp.dot(p_n.astype(BF16), vn_ref[:, cols], preferred_element_type=F32))
        o_ref[:, cols] = _diff_finish(acc / l, lam, g_ref[...], t, lam_init).astype(o_ref.dtype)


def _triangle(n):
    row = lax.broadcasted_iota(jnp.int32, (n, n), 0)
    col = lax.broadcasted_iota(jnp.int32, (n, n), 1)
    return (row > col).astype(BF16)


def _suffix_sums(x, tri):
    hi = x.astype(BF16)
    lo = (x - hi.astype(F32)).astype(BF16)
    return jnp.dot(hi, tri, preferred_element_type=F32) + jnp.dot(lo, tri, preferred_element_type=F32)


def _sb_span(q, kb, vb, r_sum, acc, scale, causal, n_parts):
    tk = kb.shape[0]
    part = tk // n_parts
    z = _nt_dot(q, kb) * scale
    sp = jnp.log(1.0 + jnp.exp(-jnp.abs(z)))
    ls = jnp.minimum(z, 0.0) - sp
    l1m = ls - z
    if causal is not None:
        l1m = jnp.where(causal, l1m, 0.0)
    tri = _triangle(part)
    between = [None] * n_parts
    for n in reversed(range(n_parts)):
        x = l1m[:, n * part:(n + 1) * part]
        between[n] = _suffix_sums(x, tri) + r_sum
        r_sum = r_sum + jnp.sum(x, axis=1, keepdims=True)
    a = jnp.exp(ls + jnp.concatenate(between, axis=1))
    if causal is not None:
        a = jnp.where(causal, a, 0.0)
    acc = acc + jnp.dot(a.astype(BF16), vb, preferred_element_type=F32)
    return r_sum, acc


def _strict_causal(tq, tk):
    row = lax.broadcasted_iota(jnp.int32, (tq, tk), 0)
    col = lax.broadcasted_iota(jnp.int32, (tq, tk), 1)
    return col < row


def _sb_parts(t):
    return max(1, t // 256)


def _sb_prompt_kernel(q_ref, k_ref, v_ref, o_in_ref, o_ref, *, t, d_b):
    del o_in_ref
    qi = pl.program_id(2)
    q = q_ref[...]
    scale = d_b ** -0.5
    dv = v_ref.shape[-1]
    parts = _sb_parts(t)
    start = pl.multiple_of(qi * t, t)
    half = t // 2
    zeros = (jnp.zeros((half, 1), F32), jnp.zeros((half, dv), F32))
    top = _sb_span(q[:half], k_ref[pl.ds(start, half), :], v_ref[pl.ds(start, half), :], *zeros,
                   scale, _strict_causal(half, half), _sb_parts(half))
    row = lax.broadcasted_iota(jnp.int32, (half, t), 0) + half
    col = lax.broadcasted_iota(jnp.int32, (half, t), 1)
    bottom = _sb_span(q[half:], k_ref[pl.ds(start, t), :], v_ref[pl.ds(start, t), :], *zeros,
                      scale, col < row, parts)
    carry = tuple(jnp.concatenate([a, b], axis=0) for a, b in zip(top, bottom))

    def body(n, carry):
        s = pl.multiple_of((qi - 1 - n) * t, t)
        return _sb_span(q, k_ref[pl.ds(s, t), :], v_ref[pl.ds(s, t), :], carry[0], carry[1],
                        scale, None, parts)

    _, acc = lax.fori_loop(0, qi, body, carry)
    o_ref[...] = acc.astype(o_ref.dtype)


def _sb_sample_kernel(q_ref, kc_ref, vc_ref, kn_ref, vn_ref, o_in_ref, o_ref, *, t, d_b, tc, hp):
    del o_in_ref
    scale = d_b ** -0.5
    past = kc_ref.shape[0]
    heads = [slice(h * d_b, (h + 1) * d_b) for h in range(hp)]
    qs = [q_ref[:, c] for c in heads]
    causal = _strict_causal(t, t)
    carry = tuple(_sb_span(qs[h], kn_ref[:, c], vn_ref[:, c], jnp.zeros((t, 1), F32),
                           jnp.zeros((t, d_b), F32), scale, causal, 1)
                  for h, c in enumerate(heads))

    def body(n, carry):
        s = pl.multiple_of(past - (n + 1) * tc, tc)
        return tuple(_sb_span(qs[h], kc_ref[pl.ds(s, tc), c].astype(BF16),
                              vc_ref[pl.ds(s, tc), c].astype(BF16), carry[h][0], carry[h][1],
                              scale, None, max(1, tc // 256))
                     for h, c in enumerate(heads))

    carry = lax.fori_loop(0, past // tc, body, carry)
    for h, c in enumerate(heads):
        o_ref[:, c] = carry[h][1].astype(o_ref.dtype)


def _prompt_attention(q, kv, bias_tiles, lam_vecs, subln_g, n_tok, batch, seq, heads, d_a, d_b, t,
                      lam_init):
    nq = seq // t
    grid = (batch, heads, nq)
    w2 = 2 * d_a
    q_spec = lambda off: pl.BlockSpec((t, w2), lambda b, h, i: (b * nq + i, off + h))
    kv_spec = pl.BlockSpec((seq, w2), lambda b, h, i: (b, h))
    o_spec = pl.BlockSpec((t, w2), lambda b, h, i: (b * nq + i, h))
    o_shape = jax.ShapeDtypeStruct((n_tok, heads * w2), BF16)
    vec = pl.BlockSpec((1, d_a), lambda b, h, i: (0, 0))
    any_spec = pl.BlockSpec(memory_space=pl.ANY)
    blank = jnp.zeros(o_shape.shape, o_shape.dtype)
    oa = pl.pallas_call(
        functools.partial(_diff_prompt_kernel, t=t, d_a=d_a, lam_init=lam_init),
        grid=grid,
        in_specs=[q_spec(0), kv_spec, kv_spec,
                  pl.BlockSpec((None, 3, t, t), lambda b, h, i: (h, 0, 0, 0)),
                  vec, vec, vec, vec,
                  pl.BlockSpec((1, w2), lambda b, h, i: (0, 0)),
                  any_spec],
        out_specs=o_spec, out_shape=o_shape,
        input_output_aliases={9: 0},
        compiler_params=_params("parallel", "parallel", "parallel"),
        name="diff_attn_prompt",
    )(q, kv[0], kv[1], bias_tiles, *lam_vecs, subln_g, blank)
    assert d_b == w2
    ob = pl.pallas_call(
        functools.partial(_sb_prompt_kernel, t=t, d_b=d_b),
        grid=grid,
        in_specs=[q_spec(heads), kv_spec, kv_spec, any_spec],
        out_specs=o_spec, out_shape=o_shape,
        input_output_aliases={3: 0},
        compiler_params=_params("parallel", "parallel", "parallel"),
        name="sb_attn_prompt",
    )(q, kv[2], kv[3], blank)
    return oa, ob


def _sample_attention(q, kv, caches, layer, bias_c, bias_n, lam_vecs, subln_g, oa, ob, n_p, db, ts,
                      heads, d_a, d_b, lam_init):
    past = caches[0].shape[2]
    w2 = 2 * d_a
    hp = 4 if heads % 4 == 0 else 1
    wg = hp * w2
    ng = heads // hp
    assert n_p % ts == 0
    row0 = n_p // ts
    grid = (db, ng)
    q_spec = lambda off: pl.BlockSpec((ts, wg), lambda b, g: (row0 + b, off + g))
    n_spec = pl.BlockSpec((ts, wg), lambda b, g: (b, g))
    c_spec = pl.BlockSpec((None, None, past, wg), lambda b, g: (layer, b, 0, g))
    o_spec = pl.BlockSpec((ts, wg), lambda b, g: (row0 + b, g))
    any_spec = pl.BlockSpec(memory_space=pl.ANY)
    vec = pl.BlockSpec((1, d_a), lambda b, g: (0, 0))
    oa = pl.pallas_call(
        functools.partial(_diff_sample_kernel, t=ts, d_a=d_a, lam_init=lam_init, hp=hp),
        grid=grid,
        in_specs=[q_spec(0), c_spec, c_spec, n_spec, n_spec,
                  pl.BlockSpec((hp, ts, past), lambda b, g: (g, 0, 0)),
                  pl.BlockSpec((hp, ts, ts), lambda b, g: (g, 0, 0)),
                  vec, vec, vec, vec,
                  pl.BlockSpec((1, w2), lambda b, g: (0, 0)),
                  any_spec],
        out_specs=o_spec, out_shape=jax.ShapeDtypeStruct(oa.shape, oa.dtype),
        input_output_aliases={12: 0},
        compiler_params=_params("parallel", "parallel"),
        name="diff_attn_sample",
    )(q, caches[0], caches[1], kv[0], kv[1], bias_c, bias_n, *lam_vecs, subln_g, oa)
    tc = _tile(past, 1024)
    ob = pl.pallas_call(
        functools.partial(_sb_sample_kernel, t=ts, d_b=d_b, tc=tc, hp=hp),
        grid=grid,
        in_specs=[q_spec(ng), c_spec, c_spec, n_spec, n_spec, any_spec],
        out_specs=o_spec, out_shape=jax.ShapeDtypeStruct(ob.shape, ob.dtype),
        input_output_aliases={5: 0},
        compiler_params=_params("parallel", "parallel"),
        name="sb_attn_sample",
    )(q, caches[2], caches[3], kv[2], kv[3], ob)
    return oa, ob


def _merge_kernel(oa_ref, ob_ref, g_ref, wa_ref, wb_ref, o_ref, *, d):
    a = jnp.dot(oa_ref[...], wa_ref[...], preferred_element_type=F32)
    b = jnp.dot(ob_ref[...], wb_ref[...], preferred_element_type=F32)
    ga = jax.nn.sigmoid(g_ref[:, :d])
    gb = jax.nn.sigmoid(g_ref[:, d:])
    o_ref[...] = (ga * a + gb * b).astype(o_ref.dtype)


def _merge(oa, ob, gates, wa, wb):
    n, w = oa.shape
    d = wa.shape[1]
    tm = _tile(n, TOKEN_TILE // 2)
    return pl.pallas_call(
        functools.partial(_merge_kernel, d=d),
        grid=(n // tm,),
        in_specs=[pl.BlockSpec((tm, w), lambda i: (i, 0)),
                  pl.BlockSpec((tm, w), lambda i: (i, 0)),
                  pl.BlockSpec((tm, 2 * d), lambda i: (i, 0)),
                  pl.BlockSpec((w, d), lambda i: (0, 0)),
                  pl.BlockSpec((w, d), lambda i: (0, 0))],
        out_specs=pl.BlockSpec((tm, d), lambda i: (i, 0)),
        out_shape=jax.ShapeDtypeStruct((n, d), BF16),
        compiler_params=_params("parallel"),
        name="branch_merge",
    )(oa, ob, gates, wa, wb)


def _out_proj_kernel(m_ref, w_ref, r_ref, g_ref, x_ref, *h_refs):
    x = jnp.dot(m_ref[...], w_ref[...], preferred_element_type=F32) + r_ref[...]
    x_ref[...] = x
    y = _rms(x, g_ref[...])
    for h_ref in h_refs:
        h_ref[...] = y.astype(h_ref.dtype)


def _out_proj(merged, w, res, g, norm_dtypes):
    n, d = res.shape
    k = merged.shape[1]
    tm = _tile(n, TOKEN_TILE // 2)
    row = lambda c: pl.BlockSpec((tm, c), lambda i: (i, 0))
    return pl.pallas_call(
        _out_proj_kernel,
        grid=(n // tm,),
        in_specs=[row(k), pl.BlockSpec((k, d), lambda i: (0, 0)), row(d),
                  pl.BlockSpec((1, d), lambda i: (0, 0))],
        out_specs=[row(d)] + [row(d) for _ in norm_dtypes],
        out_shape=[jax.ShapeDtypeStruct((n, d), F32)]
                  + [jax.ShapeDtypeStruct((n, d), dt) for dt in norm_dtypes],
        compiler_params=_params("parallel"),
        name="proj_out",
    )(merged, w, res, g.reshape(1, d).astype(F32))


def _ffn_kernel(te_ref, tv_ref, x_ref, wg_ref, wu_ref, wd_ref, *rest, has_res):
    if has_res:
        res_ref, o_ref, xb_ref, acc_ref = rest
    else:
        res_ref = None
        o_ref, xb_ref, acc_ref = rest
    i = pl.program_id(0)
    j = pl.program_id(1)
    last = pl.num_programs(1) - 1
    valid = tv_ref[i] > 0

    @pl.when(jnp.logical_and(valid, j == 0))
    def _():
        xb_ref[...] = x_ref[...].astype(BF16)
        acc_ref[...] = jnp.zeros_like(acc_ref)

    @pl.when(valid)
    def _():
        x = xb_ref[...]
        g = jnp.dot(x, wg_ref[...], preferred_element_type=F32)
        u = jnp.dot(x, wu_ref[...], preferred_element_type=F32)
        a = (g * jax.nn.sigmoid(g)) * u
        acc_ref[...] += jnp.dot(a.astype(BF16), wd_ref[...], preferred_element_type=F32)

    @pl.when(jnp.logical_and(valid, j == last))
    def _():
        out = acc_ref[...]
        if has_res:
            out = out + res_ref[...]
        o_ref[...] = out

    @pl.when(jnp.logical_and(jnp.logical_not(valid), j == last))
    def _():
        o_ref[...] = jnp.zeros_like(o_ref)


def _ffn(x, wg, wu, wd, tile_expert, tile_valid, tm, name, res=None):
    n, d = x.shape
    f = wg.shape[2]
    tf = _tile(f, FF_TILE)
    nj = f // tf

    def fj(i, j, tv):
        return jnp.where(tv[i] > 0, j, nj - 1)

    in_specs = [pl.BlockSpec((tm, d), lambda i, j, te, tv: (i, 0)),
                pl.BlockSpec((None, d, tf), lambda i, j, te, tv: (te[i], 0, fj(i, j, tv))),
                pl.BlockSpec((None, d, tf), lambda i, j, te, tv: (te[i], 0, fj(i, j, tv))),
                pl.BlockSpec((None, tf, d), lambda i, j, te, tv: (te[i], fj(i, j, tv), 0))]
    args = [x, wg, wu, wd]
    if res is not None:
        in_specs.append(pl.BlockSpec((tm, d), lambda i, j, te, tv: (i, 0)))
        args.append(res)
    return pl.pallas_call(
        functools.partial(_ffn_kernel, has_res=res is not None),
        grid_spec=pltpu.PrefetchScalarGridSpec(
            num_scalar_prefetch=2,
            grid=(n // tm, nj),
            in_specs=in_specs,
            out_specs=pl.BlockSpec((tm, d), lambda i, j, te, tv: (i, 0)),
            scratch_shapes=[pltpu.VMEM((tm, d), BF16), pltpu.VMEM((tm, d), F32)]),
        out_shape=jax.ShapeDtypeStruct((n, d), F32),
        compiler_params=_params("parallel", "arbitrary"),
        name=name,
    )(tile_expert, tile_valid, *args)


def _router_kernel(h_ref, w_ref, b_ref, idx_ref, gate_ref, cnt_ref, run_ref, *, tm):
    i = pl.program_id(0)

    @pl.when(i == 0)
    def _():
        run_ref[...] = jnp.zeros_like(run_ref)

    logits = jnp.dot(h_ref[...], w_ref[...], preferred_element_type=F32) + b_ref[...]
    lane = lax.broadcasted_iota(jnp.int32, logits.shape, 1)
    m1 = jnp.max(logits, axis=1, keepdims=True)
    i1 = jnp.min(jnp.where(logits == m1, lane, LANES), axis=1, keepdims=True)
    oh1 = lane == i1
    rest = jnp.where(oh1, -jnp.inf, logits)
    m2 = jnp.max(rest, axis=1, keepdims=True)
    i2 = jnp.min(jnp.where(rest == m2, lane, LANES), axis=1, keepdims=True)
    oh2 = lane == i2
    e = jnp.exp(m2 - m1)
    g1 = 1.0 / (1.0 + e)
    g2 = e / (1.0 + e)

    row = lax.broadcasted_iota(jnp.int32, (tm, tm), 0)
    col = lax.broadcasted_iota(jnp.int32, (tm, tm), 1)
    tri = (col < row).astype(BF16)
    picked = jnp.logical_or(oh1, oh2).astype(BF16)
    before = jnp.dot(tri, picked, preferred_element_type=F32) + run_ref[...]
    r1 = jnp.sum(jnp.where(oh1, before, 0.0), axis=1, keepdims=True).astype(jnp.int32)
    r2 = jnp.sum(jnp.where(oh2, before, 0.0), axis=1, keepdims=True).astype(jnp.int32)
    run_ref[...] += jnp.sum(picked.astype(F32), axis=0, keepdims=True)

    idx_ref[...] = jnp.where(lane == 0, i1, jnp.where(lane == 1, i2,
                             jnp.where(lane == 2, r1, jnp.where(lane == 3, r2, 0))))
    gate_ref[...] = jnp.where(lane == 0, g1, jnp.where(lane == 1, g2, 0.0))
    cnt_ref[...] = run_ref[...]


def _router(h, w_r, b_r):
    n, d = h.shape
    ne = w_r.shape[1]
    tm = _tile(n, TOKEN_TILE)
    w_pad = jnp.zeros((d, LANES), BF16).at[:, :ne].set(w_r.astype(BF16))
    b_pad = jnp.full((1, LANES), NEG_INF, F32).at[0, :ne].set(b_r.astype(F32))
    return pl.pallas_call(
        functools.partial(_router_kernel, tm=tm),
        grid=(n // tm,),
        in_specs=[pl.BlockSpec((tm, d), lambda i: (i, 0)),
                  pl.BlockSpec((d, LANES), lambda i: (0, 0)),
                  pl.BlockSpec((1, LANES), lambda i: (0, 0))],
        out_specs=[pl.BlockSpec((tm, LANES), lambda i: (i, 0)),
                   pl.BlockSpec((tm, LANES), lambda i: (i, 0)),
                   pl.BlockSpec((1, LANES), lambda i: (0, 0))],
        out_shape=[jax.ShapeDtypeStruct((n, LANES), jnp.int32),
                   jax.ShapeDtypeStruct((n, LANES), F32),
                   jax.ShapeDtypeStruct((1, LANES), F32)],
        scratch_shapes=[pltpu.VMEM((1, LANES), F32)],
        compiler_params=_params("arbitrary"),
        name="moe_router",
    )(h, w_pad, b_pad)


def _row_copy(src, src_row, dst, dst_row, sem):
    return pltpu.make_async_copy(src.at[pl.ds(src_row, 1), :], dst.at[pl.ds(dst_row, 1), :], sem)


def _dispatch_kernel(src_ref, h_ref, o_ref, buf_ref, sem, *, tt):
    def issue(r, c):
        _row_copy(h_ref, src_ref[0, 0, r], buf_ref, r, sem).start()
        return c

    lax.fori_loop(0, tt, issue, 0, unroll=8)

    def drain(r, c):
        _row_copy(h_ref, 0, buf_ref, 0, sem).wait()
        return c

    lax.fori_loop(0, tt, drain, 0, unroll=8)
    o_ref[...] = buf_ref[...].astype(o_ref.dtype)


def _dispatch(h, src):
    d = h.shape[1]
    n_slots = src.shape[0]
    tt = _tile(n_slots, ROW_DMA_BYTES // (d * h.dtype.itemsize))
    return pl.pallas_call(
        functools.partial(_dispatch_kernel, tt=tt),
        grid=(n_slots // tt,),
        in_specs=[pl.BlockSpec((1, 1, tt), lambda i: (i, 0, 0), memory_space=pltpu.SMEM),
                  pl.BlockSpec(memory_space=pl.ANY)],
        out_specs=pl.BlockSpec((tt, d), lambda i: (i, 0)),
        out_shape=jax.ShapeDtypeStruct((n_slots, d), BF16),
        scratch_shapes=[pltpu.VMEM((tt, d), h.dtype), pltpu.SemaphoreType.DMA(())],
        compiler_params=_params("arbitrary"),
        name="moe_dispatch",
    )(src.reshape(n_slots // tt, 1, tt), h)


def _combine_kernel(pos_ref, x_ref, gate_ref, y_ref, g_ref, *rest, tt, want_x):
    if want_x:
        x_out_ref, o_ref, buf_ref, sem = rest
    else:
        o_ref, buf_ref, sem = rest

    def issue(r, c):
        for k in range(TOP_K):
            _row_copy(y_ref, pos_ref[0, 0, TOP_K * r + k], buf_ref.at[k], r, sem).start()
        return c

    lax.fori_loop(0, tt, issue, 0, unroll=8)

    def drain(r, c):
        for k in range(TOP_K):
            _row_copy(y_ref, 0, buf_ref.at[k], 0, sem).wait()
        return c

    lax.fori_loop(0, tt, drain, 0, unroll=8)
    g = gate_ref[...]
    x = x_ref[...] + (g[:, 0:1] * buf_ref[0] + g[:, 1:2] * buf_ref[1])
    if want_x:
        x_out_ref[...] = x
    o_ref[...] = _rms(x, g_ref[...]).astype(o_ref.dtype)


def _combine_norm(x, gates, y, pos, g, row0, n_rows, norm_dtype, want_x):
    n, d = x.shape
    tt = _tile(n_rows, ROW_DMA_BYTES // (TOP_K * d * y.dtype.itemsize))
    assert n % tt == 0 and row0 % tt == 0
    off = row0 // tt
    pos3 = pos.reshape(n // tt, 1, TOP_K * tt)
    out = pl.BlockSpec((tt, d), lambda i: (i, 0))
    return pl.pallas_call(
        functools.partial(_combine_kernel, tt=tt, want_x=want_x),
        grid=(n_rows // tt,),
        in_specs=[pl.BlockSpec((1, 1, TOP_K * tt), lambda i: (off + i, 0, 0), memory_space=pltpu.SMEM),
                  pl.BlockSpec((tt, d), lambda i: (off + i, 0)),
                  pl.BlockSpec((tt, LANES), lambda i: (off + i, 0)),
                  pl.BlockSpec(memory_space=pl.ANY),
                  pl.BlockSpec((1, d), lambda i: (0, 0))],
        out_specs=[out] * (2 if want_x else 1),
        out_shape=([jax.ShapeDtypeStruct((n_rows, d), F32)] if want_x else [])
                  + [jax.ShapeDtypeStruct((n_rows, d), norm_dtype)],
        scratch_shapes=[pltpu.VMEM((TOP_K, tt, d), F32), pltpu.SemaphoreType.DMA(())],
        compiler_params=_params("arbitrary"),
        name="moe_combine",
    )(pos3, x, gates, y, g.reshape(1, d).astype(F32))


def _moe_route(h_bf, w_r, b_r, ne, tm):
    n = h_bf.shape[0]
    idx, gates, counts = _router(h_bf, w_r, b_r)
    counts = counts[0, :ne].astype(jnp.int32)
    padded = ((counts + tm - 1) // tm) * tm
    ends = jnp.cumsum(padded)
    offsets = ends - padded
    expert = idx[:, :TOP_K]
    pos = jnp.zeros_like(expert)
    for e in range(ne):
        pos = jnp.where(expert == e, offsets[e], pos)
    pos = pos + idx[:, TOP_K:2 * TOP_K]
    n_tiles = (n * TOP_K) // tm + ne
    tile_start = jnp.arange(n_tiles, dtype=jnp.int32) * tm
    tile_valid = (tile_start < ends[-1]).astype(jnp.int32)
    tile_expert = jnp.minimum(jnp.sum((tile_start[:, None] >= ends[None, :]).astype(jnp.int32), axis=1),
                              ne - 1)
    last_expert = jnp.max(jnp.where(counts > 0, jnp.arange(ne, dtype=jnp.int32), 0))
    tile_expert = jnp.where(tile_valid > 0, tile_expert, last_expert)
    token = jnp.broadcast_to(jnp.arange(n, dtype=jnp.int32)[:, None], pos.shape)
    src = jnp.zeros((n_tiles * tm,), jnp.int32).at[pos.reshape(-1)].set(
        token.reshape(-1), unique_indices=True)
    return pos, src, gates, tile_expert, tile_valid


def kernel(x_prompt, x_sample, cache_diff_k, cache_diff_v, cache_sb_k, cache_sb_v, rel_bias, norm_mix_g, w_in, lambda_q1, lambda_k1, lambda_q2, lambda_k2, subln_g, w_branch_a, w_branch_b, w_out, norm_ffn_g, dense_w_gate, dense_w_up, dense_w_down, router_w, router_b, moe_w_gate, moe_w_up, moe_w_down, final_norm_g):
    batch, seq, d = x_prompt.shape
    db, ts, _ = x_sample.shape
    depth, _, past, heads, w2 = cache_diff_k.shape
    d_a = w2 // 2
    d_b = cache_sb_k.shape[-1]
    wid = heads * w2
    n_p = batch * seq
    n_s = db * ts
    n_tok = n_p + n_s
    assert d % wid == 0 and w_in.shape[2] == 6 * wid + 2 * d
    assert cache_sb_k.shape[3] * d_b == wid and cache_diff_v.shape[-1] == w2

    x, h = _join_norm(x_prompt.reshape(n_p, d), x_sample.reshape(n_s, d), norm_mix_g[0])

    t_attn = _tile(seq, ATTN_TILE)
    pos_q = past + jnp.arange(ts, dtype=jnp.int32)
    bias_tiles = _prompt_bias_tiles(rel_bias, t_attn)
    bias_c = _masked_bias(rel_bias, pos_q, jnp.arange(past, dtype=jnp.int32))
    bias_n = _masked_bias(rel_bias, pos_q, pos_q)

    caches = [c.astype(BF16).reshape(depth, db, past, wid)
              for c in (cache_diff_k, cache_diff_v, cache_sb_k, cache_sb_v)]
    cache_p = [None] * 4
    cache_s = [None] * 4
    kv_cols = (1, 2, 4, 5)
    for l in range(depth):
        lam_init = 0.8 - 0.6 * math.exp(-0.3 * l)
        (q,) = _proj(h, w_in, l, lambda j: 3 * j, 2, wid,
                     [((n_tok, 2 * wid), BF16, (_tile(n_tok, TOKEN_TILE), wid), lambda j, i: (i, j))],
                     "proj_q")
        (gates,) = _proj(h, w_in, l, lambda j: j + 6, 2 * d // wid, wid,
                         [((n_tok, 2 * d), F32, (_tile(n_tok, TOKEN_TILE), wid), lambda j, i: (i, j))],
                         "proj_gate")
        kv_p, kv_s = [], []
        for c in range(4):
            cache_p[c], b16 = _proj_cache_rows(h, w_in, l, kv_cols[c], wid, 0, n_p, cache_p[c], "proj_kv_prompt")
            kv_p.append(b16)
            cache_s[c], b16 = _proj_cache_rows(h, w_in, l, kv_cols[c], wid, n_p, n_s, cache_s[c], "proj_kv_sample")
            kv_s.append(b16)

        lam_vecs = [v[l].reshape(1, d_a).astype(F32) for v in (lambda_q1, lambda_k1, lambda_q2, lambda_k2)]
        g_sub = subln_g[l].reshape(1, w2).astype(F32)
        oa, ob = _prompt_attention(q, kv_p, bias_tiles, lam_vecs, g_sub, n_tok, batch, seq, heads, d_a, d_b,
                                   t_attn, lam_init)
        oa, ob = _sample_attention(q, kv_s, caches, l, bias_c, bias_n, lam_vecs, g_sub, oa, ob, n_p, db, ts,
                                   heads, d_a, d_b, lam_init)
        merged = _merge(oa, ob, gates, w_branch_a[l].astype(BF16), w_branch_b[l].astype(BF16))

        j = l // 2
        last = l + 1 == depth
        tm = _tile(n_tok, TOKEN_TILE)
        if l % 2 == 0:
            x, h2 = _out_proj(merged, w_out[l].astype(BF16), x, norm_ffn_g[l], [BF16])
            n_tiles = n_tok // tm
            x = _ffn(h2, dense_w_gate[j:j + 1].astype(BF16), dense_w_up[j:j + 1].astype(BF16),
                     dense_w_down[j:j + 1].astype(BF16), jnp.zeros((n_tiles,), jnp.int32),
                     jnp.ones((n_tiles,), jnp.int32), tm, "dense_ffn", res=x)
            if last:
                (y_prompt,) = _rmsnorm(x, final_norm_g, [F32], "norm_final", 0, n_p)
                (y_sample,) = _rmsnorm(x, final_norm_g, [F32], "norm_final", n_p, n_s)
            else:
                (h,) = _rmsnorm(x, norm_mix_g[l + 1], [BF16], "norm_mix")
        else:
            x, h2, h2_f32 = _out_proj(merged, w_out[l].astype(BF16), x, norm_ffn_g[l], [BF16, F32])
            ne = moe_w_gate.shape[1]
            pos, src, route_g, tile_expert, tile_valid = _moe_route(h2, router_w[j], router_b[j], ne, tm)
            xs = _dispatch(h2_f32, src)
            y = _ffn(xs, moe_w_gate[j].astype(BF16), moe_w_up[j].astype(BF16), moe_w_down[j].astype(BF16),
                     tile_expert, tile_valid, tm, "moe_ffn")
            if last:
                (y_prompt,) = _combine_norm(x, route_g, y, pos, final_norm_g, 0, n_p, F32, False)
                (y_sample,) = _combine_norm(x, route_g, y, pos, final_norm_g, n_p, n_s, F32, False)
            else:
                x, h = _combine_norm(x, route_g, y, pos, norm_mix_g[l + 1], 0, n_tok, BF16, True)

    outs_p = [c.reshape(depth, batch, seq, heads, w2) for c in cache_p]
    outs_s = [c.reshape(depth, db, ts, heads, w2) for c in cache_s]
    return (y_prompt.reshape(batch, seq, d), y_sample.reshape(db, ts, d), *outs_p, *outs_s)
```

```python
import functools
import math

import jax
import jax.numpy as jnp
from jax import lax
from jax.experimental import pallas as pl
from jax.experimental.pallas import tpu as pltpu

CHUNK = 64
NUM_BUCKETS = 32
MAX_DISTANCE = 128
TOP_K = 2
EPS = 1e-6
NEG_INF = -1e30

LANES = 128
VMEM_LIMIT_BYTES = 56 * 1024 * 1024
TOKEN_TILE = 512
ATTN_TILE = 512
FF_TILE = 512
ROW_DMA_BYTES = 4 * 1024 * 1024

BF16 = jnp.bfloat16
F32 = jnp.float32


def _params(*sem):
    return pltpu.CompilerParams(dimension_semantics=sem, vmem_limit_bytes=VMEM_LIMIT_BYTES)


def _tile(n, want):
    t = min(n, want)
    while n % t:
        t //= 2
    return t


def _rms(x, g):
    return x * lax.rsqrt(jnp.mean(x * x, axis=-1, keepdims=True) + EPS) * g


def _rmsnorm_kernel(x_ref, g_ref, *o_refs):
    y = _rms(x_ref[...].astype(F32), g_ref[...])
    for o_ref in o_refs:
        o_ref[...] = y.astype(o_ref.dtype)


def _rmsnorm(x, g, out_dtypes, name, row0=0, n_rows=None):
    d = x.shape[1]
    n_rows = x.shape[0] if n_rows is None else n_rows
    tm = _tile(n_rows, TOKEN_TILE)
    assert row0 % tm == 0
    off = row0 // tm
    return pl.pallas_call(
        _rmsnorm_kernel,
        grid=(n_rows // tm,),
        in_specs=[pl.BlockSpec((tm, d), lambda i: (off + i, 0)),
                  pl.BlockSpec((1, d), lambda i: (0, 0))],
        out_specs=[pl.BlockSpec((tm, d), lambda i: (i, 0)) for _ in out_dtypes],
        out_shape=[jax.ShapeDtypeStruct((n_rows, d), dt) for dt in out_dtypes],
        compiler_params=_params("parallel"),
        name=name,
    )(x, g.reshape(1, d).astype(F32))


def _join_norm_kernel(xp_ref, xs_ref, g_ref, x_ref, h_ref, *, np_tiles):
    x = jnp.where(pl.program_id(0) < np_tiles, xp_ref[...], xs_ref[...])
    x_ref[...] = x
    h_ref[...] = _rms(x, g_ref[...]).astype(h_ref.dtype)


def _join_norm(xp, xs, g):
    (n_p, d), n_s = xp.shape, xs.shape[0]
    tm = _tile(n_s, TOKEN_TILE)
    assert n_p % tm == 0
    np_tiles = n_p // tm
    out = pl.BlockSpec((tm, d), lambda i: (i, 0))
    return pl.pallas_call(
        functools.partial(_join_norm_kernel, np_tiles=np_tiles),
        grid=((n_p + n_s) // tm,),
        in_specs=[pl.BlockSpec((tm, d), lambda i: (jnp.minimum(i, np_tiles - 1), 0)),
                  pl.BlockSpec((tm, d), lambda i: (jnp.maximum(i - np_tiles, 0), 0)),
                  pl.BlockSpec((1, d), lambda i: (0, 0))],
        out_specs=[out, out],
        out_shape=[jax.ShapeDtypeStruct((n_p + n_s, d), F32), jax.ShapeDtypeStruct((n_p + n_s, d), BF16)],
        compiler_params=_params("parallel"),
        name="join_norm",
    )(xp, xs, g.reshape(1, d).astype(F32))


def _proj_kernel(x_ref, w_ref, *rest, n_out):
    o_refs, wb_ref = rest[-1 - n_out:-1], rest[-1]

    @pl.when(pl.program_id(1) == 0)
    def _():
        wb_ref[...] = w_ref[...].astype(BF16)

    acc = jnp.dot(x_ref[...], wb_ref[...], preferred_element_type=F32)
    for o_ref in o_refs:
        o_ref[...] = acc.astype(o_ref.dtype)


def _proj(x, w, layer, col_block_fn, n_blocks, tn, outs, name, row0=0, n_rows=None):
    k = x.shape[1]
    n_rows = x.shape[0] if n_rows is None else n_rows
    tm = _tile(n_rows, TOKEN_TILE)
    assert row0 % tm == 0
    off = row0 // tm
    return pl.pallas_call(
        functools.partial(_proj_kernel, n_out=len(outs)),
        grid=(n_blocks, n_rows // tm),
        in_specs=[pl.BlockSpec((tm, k), lambda j, i: (off + i, 0)),
                  pl.BlockSpec((None, k, tn), lambda j, i: (layer, 0, col_block_fn(j)))],
        out_specs=[pl.BlockSpec(o[2], o[3]) for o in outs],
        out_shape=[jax.ShapeDtypeStruct(o[0], o[1]) for o in outs],
        scratch_shapes=[pltpu.VMEM((k, tn), BF16)],
        compiler_params=_params("parallel", "arbitrary"),
        name=name,
    )(x, w)


def _cache_rows_kernel(x_ref, w_ref, *rest):
    o_ref, wb_ref = rest[-2:]
    s = pl.program_id(0)

    @pl.when(jnp.logical_and(s == 0, pl.program_id(1) == 0))
    def _():
        wb_ref[...] = w_ref[...].astype(BF16)

    @pl.when(s == 0)
    def _():
        o_ref[...] = jnp.dot(x_ref[...], wb_ref[...], preferred_element_type=F32)

    @pl.when(s > 0)
    def _():
        o_ref[...] = jnp.zeros_like(o_ref)


def _proj_cache_rows(h, w_in, layer, col, wid, row0, n_rows, prev, name):
    depth, k, _ = w_in.shape
    tm = _tile(n_rows, TOKEN_TILE)
    assert row0 % tm == 0
    off = row0 // tm
    nt = n_rows // tm
    tile = lambda s, i: jnp.where(s == 0, i, nt - 1)
    in_specs = [pl.BlockSpec((tm, k), lambda s, i: (off + tile(s, i), 0)),
                pl.BlockSpec((None, k, wid), lambda s, i: (layer, 0, col))]
    args = [h, w_in]
    if prev is not None:
        in_specs.append(pl.BlockSpec(memory_space=pl.ANY))
        args.append(prev)
    return pl.pallas_call(
        _cache_rows_kernel,
        grid=(1 if prev is not None else depth, nt),
        in_specs=in_specs,
        out_specs=pl.BlockSpec((None, tm, wid), lambda s, i: ((layer + s) % depth, i, 0)),
        out_shape=jax.ShapeDtypeStruct((depth, n_rows, wid), F32),
        scratch_shapes=[pltpu.VMEM((k, wid), BF16)],
        input_output_aliases={2: 0} if prev is not None else {},
        compiler_params=_params("arbitrary", "arbitrary"),
        name=name,
    )(*args)


def _t5_bucket(rel):
    half = NUM_BUCKETS // 2
    max_exact = half // 2
    n = jnp.abs(rel)
    far = max_exact + (jnp.log(jnp.maximum(n, 1).astype(F32) / max_exact)
                       / math.log(MAX_DISTANCE / max_exact) * (half - max_exact)).astype(jnp.int32)
    far = jnp.minimum(far, half - 1)
    return jnp.where(rel > 0, half, 0) + jnp.where(n < max_exact, n, far)


def _masked_bias(rel_bias, q_pos, k_pos):
    bucket = _t5_bucket(k_pos[None, :] - q_pos[:, None])
    bias = jnp.zeros((rel_bias.shape[1],) + bucket.shape, F32)
    for b in range(NUM_BUCKETS):
        bias = jnp.where(bucket[None] == b, rel_bias[b].astype(F32)[:, None, None], bias)
    mask = (k_pos[None, :] // CHUNK) <= (q_pos[:, None] // CHUNK)
    return jnp.where(mask[None], bias, NEG_INF)


def _prompt_bias_tiles(rel_bias, t):
    assert t >= MAX_DISTANCE and t % CHUNK == 0
    pos = jnp.arange(t, dtype=jnp.int32)
    tiles = [_masked_bias(rel_bias, 2 * t + pos, (2 - d) * t + pos) for d in (2, 1, 0)]
    return jnp.stack(tiles, axis=1)


def _split_maps(q, d_a):
    lane = lax.broadcasted_iota(jnp.int32, q.shape, 1)
    zero = jnp.zeros_like(q)
    return jnp.concatenate([jnp.where(lane < d_a, q, zero), jnp.where(lane >= d_a, q, zero)], axis=0)


def _lambda(lq1, lk1, lq2, lk2, lam_init):
    return (jnp.exp(jnp.sum(lq1 * lk1, axis=-1, keepdims=True))
            - jnp.exp(jnp.sum(lq2 * lk2, axis=-1, keepdims=True)) + lam_init)


def _diff_finish(o, lam, g, t, lam_init):
    oa = o[:t] - lam * o[t:]
    return _rms(oa, g) * (1.0 - lam_init)


def _nt_dot(a, b):
    return lax.dot_general(a, b, (((1,), (1,)), ((), ())), preferred_element_type=F32)


def _diff_prompt_kernel(q_ref, k_ref, v_ref, bias_ref, lq1_ref, lk1_ref, lq2_ref, lk2_ref, g_ref,
                        o_in_ref, o_ref, *, t, d_a, lam_init):
    del o_in_ref
    qi = pl.program_id(2)
    scale = d_a ** -0.5
    q = q_ref[...]
    exact_scale = math.frexp(scale)[0] == 0.5
    if exact_scale:
        q = (q.astype(F32) * scale).astype(q.dtype)
    qq = _split_maps(q, d_a)
    dv = v_ref.shape[-1]
    half = t // 2
    assert half % CHUNK == 0

    def update(sc, bias, state, vb):
        m, l, acc = state
        sc = (sc if exact_scale else sc * scale) + bias
        m_new = jnp.maximum(m, jnp.max(sc, axis=1, keepdims=True))
        alpha = jnp.exp(m - m_new)
        p = jnp.exp(sc - m_new)
        l = alpha * l + jnp.sum(p, axis=1, keepdims=True)
        acc = alpha * acc + jnp.dot(p.astype(BF16), vb, preferred_element_type=F32)
        return m_new, l, acc

    def body(j, carry):
        start = pl.multiple_of(j * t, t)
        vb = v_ref[pl.ds(start, t), :].astype(BF16)
        bt = bias_ref[jnp.clip(j - qi + 2, 0, 1)]
        s = _nt_dot(qq, k_ref[pl.ds(start, t), :].astype(BF16))
        return tuple(update(s[c * t:(c + 1) * t], bt, carry[c], vb) for c in range(2))

    one = (jnp.full((t, 1), NEG_INF, F32), jnp.zeros((t, 1), F32), jnp.zeros((t, dv), F32))
    carry = lax.fori_loop(0, qi, body, (one, one))

    start = pl.multiple_of(qi * t, t)
    bt = bias_ref[2]
    rows = lambda x, c, r: x[c * t + r * half:c * t + (r + 1) * half]
    q_top = jnp.concatenate([rows(qq, 0, 0), rows(qq, 1, 0)], axis=0)
    q_bot = jnp.concatenate([rows(qq, 0, 1), rows(qq, 1, 1)], axis=0)
    s_top = _nt_dot(q_top, k_ref[pl.ds(start, half), :].astype(BF16))
    s_bot = _nt_dot(q_bot, k_ref[pl.ds(start, t), :].astype(BF16))
    outs = []
    for c in range(2):
        state = [tuple(x[r * half:(r + 1) * half] for x in carry[c]) for r in range(2)]
        top = update(s_top[c * half:(c + 1) * half], bt[:half, :half], state[0],
                     v_ref[pl.ds(start, half), :].astype(BF16))
        bot = update(s_bot[c * half:(c + 1) * half], bt[half:, :], state[1], v_ref[pl.ds(start, t), :].astype(BF16))
        outs.append(jnp.concatenate([top[2] / top[1], bot[2] / bot[1]], axis=0))
    lam = _lambda(lq1_ref[...], lk1_ref[...], lq2_ref[...], lk2_ref[...], lam_init)
    o = jnp.concatenate(outs, axis=0)
    o_ref[...] = _diff_finish(o, lam, g_ref[...], t, lam_init).astype(o_ref.dtype)


def _head_rows(ref, h, heads, start, size):
    return ref[pl.ds(start * heads + h, size, stride=heads), :]


def _diff_sample_kernel(q_ref, kc_ref, vc_ref, kn_ref, vn_ref, bc_ref, bn_ref, lq1_ref, lk1_ref,
                        lq2_ref, lk2_ref, g_ref, o_in_ref, o_ref, *, t, d_a, lam_init, heads):
    del o_in_ref
    scale = d_a ** -0.5
    w2 = 2 * d_a
    past = kc_ref.shape[0] // heads
    lam = _lambda(lq1_ref[...], lk1_ref[...], lq2_ref[...], lk2_ref[...], lam_init)
    for h in range(heads):
        cols = slice(h * w2, (h + 1) * w2)
        qq = _split_maps(q_ref[:, cols], d_a)
        bc = bc_ref[h]
        bn = bn_ref[h]
        kc = _head_rows(kc_ref, h, heads, 0, past).astype(BF16)
        vc = _head_rows(vc_ref, h, heads, 0, past).astype(BF16)
        s_c = _nt_dot(qq, kc) * scale + jnp.concatenate([bc, bc], axis=0)
        s_n = _nt_dot(qq, kn_ref[:, cols].astype(BF16)) * scale + jnp.concatenate([bn, bn], axis=0)
        m = jnp.maximum(jnp.max(s_c, axis=1, keepdims=True), jnp.max(s_n, axis=1, keepdims=True))
        p_c = jnp.exp(s_c - m)
        p_n = jnp.exp(s_n - m)
        l = jnp.sum(p_c, axis=1, keepdims=True) + jnp.sum(p_n, axis=1, keepdims=True)
        acc = (jnp.dot(p_c.astype(BF16), vc, preferred_element_type=F32)
               + jnp.dot(p_n.astype(BF16), vn_ref[:, cols].astype(BF16), preferred_element_type=F32))
        o_ref[:, cols] = _diff_finish(acc / l, lam, g_ref[...], t, lam_init).astype(o_ref.dtype)


def _triangle(n):
    row = lax.broadcasted_iota(jnp.int32, (n, n), 0)
    col = lax.broadcasted_iota(jnp.int32, (n, n), 1)
    return (row > col).astype(BF16)


def _suffix_sums(x, tri):
    hi = x.astype(BF16)
    lo = (x - hi.astype(F32)).astype(BF16)
    return jnp.dot(hi, tri, preferred_element_type=F32) + jnp.dot(lo, tri, preferred_element_type=F32)


def _sb_span(q, kb, vb, r_sum, acc, scale, causal, n_parts):
    tk = kb.shape[0]
    part = tk // n_parts
    z = _nt_dot(q, kb) * scale
    sp = jnp.log(1.0 + jnp.exp(-jnp.abs(z)))
    ls = jnp.minimum(z, 0.0) - sp
    l1m = ls - z
    if causal is not None:
        l1m = jnp.where(causal, l1m, 0.0)
    tri = _triangle(part)
    between = [None] * n_parts
    for n in reversed(range(n_parts)):
        x = l1m[:, n * part:(n + 1) * part]
        between[n] = _suffix_sums(x, tri) + r_sum
        r_sum = r_sum + jnp.sum(x, axis=1, keepdims=True)
    a = jnp.exp(ls + jnp.concatenate(between, axis=1))
    if causal is not None:
        a = jnp.where(causal, a, 0.0)
    acc = acc + jnp.dot(a.astype(BF16), vb, preferred_element_type=F32)
    return r_sum, acc


def _strict_causal(tq, tk):
    row = lax.broadcasted_iota(jnp.int32, (tq, tk), 0)
    col = lax.broadcasted_iota(jnp.int32, (tq, tk), 1)
    return col < row


def _sb_parts(t):
    return max(1, t // 256)


def _sb_prompt_kernel(q_ref, k_ref, v_ref, o_in_ref, o_ref, *, t, d_b):
    del o_in_ref
    qi = pl.program_id(2)
    q = q_ref[...]
    scale = d_b ** -0.5
    dv = v_ref.shape[-1]
    parts = _sb_parts(t)
    start = pl.multiple_of(qi * t, t)
    half = t // 2
    zeros = (jnp.zeros((half, 1), F32), jnp.zeros((half, dv), F32))
    top = _sb_span(q[:half], k_ref[pl.ds(start, half), :].astype(BF16), v_ref[pl.ds(start, half), :].astype(BF16), *zeros,
                   scale, _strict_causal(half, half), _sb_parts(half))
    row = lax.broadcasted_iota(jnp.int32, (half, t), 0) + half
    col = lax.broadcasted_iota(jnp.int32, (half, t), 1)
    bottom = _sb_span(q[half:], k_ref[pl.ds(start, t), :].astype(BF16), v_ref[pl.ds(start, t), :].astype(BF16), *zeros,
                      scale, col < row, parts)
    carry = tuple(jnp.concatenate([a, b], axis=0) for a, b in zip(top, bottom))

    def body(n, carry):
        s = pl.multiple_of((qi - 1 - n) * t, t)
        return _sb_span(q, k_ref[pl.ds(s, t), :].astype(BF16), v_ref[pl.ds(s, t), :].astype(BF16),
                        carry[0], carry[1],
                        scale, None, parts)

    _, acc = lax.fori_loop(0, qi, body, carry)
    o_ref[...] = acc.astype(o_ref.dtype)


def _sb_sample_kernel(q_ref, kc_ref, vc_ref, kn_ref, vn_ref, o_in_ref, o_ref, *, t, d_b, tc, heads):
    del o_in_ref
    scale = d_b ** -0.5
    past = kc_ref.shape[0] // heads
    cols = [slice(h * d_b, (h + 1) * d_b) for h in range(heads)]
    qs = [q_ref[:, c] for c in cols]
    causal = _strict_causal(t, t)
    carry = tuple(_sb_span(qs[h], kn_ref[:, c].astype(BF16), vn_ref[:, c].astype(BF16),
                           jnp.zeros((t, 1), F32), jnp.zeros((t, d_b), F32), scale, causal, 1)
                  for h, c in enumerate(cols))

    def body(n, carry):
        s = pl.multiple_of(past - (n + 1) * tc, tc)
        return tuple(_sb_span(qs[h], _head_rows(kc_ref, h, heads, s, tc).astype(BF16),
                              _head_rows(vc_ref, h, heads, s, tc).astype(BF16), carry[h][0], carry[h][1],
                              scale, None, _sb_parts(tc))
                     for h in range(heads))

    carry = lax.fori_loop(0, past // tc, body, carry)
    for h, c in enumerate(cols):
        o_ref[:, c] = carry[h][1].astype(o_ref.dtype)


def _prompt_attention(q, kv, layer, bias_tiles, lam_vecs, subln_g, n_tok, batch, seq, heads, d_a, d_b, t,
                      lam_init):
    nq = seq // t
    grid = (batch, heads, nq)
    w2 = 2 * d_a
    q_spec = lambda off: pl.BlockSpec((t, w2), lambda b, h, i: (b * nq + i, off + h))
    kv_spec = pl.BlockSpec((None, seq, w2), lambda b, h, i: (layer, b, h))
    o_spec = pl.BlockSpec((t, w2), lambda b, h, i: (b * nq + i, h))
    o_shape = jax.ShapeDtypeStruct((n_tok, heads * w2), BF16)
    vec = pl.BlockSpec((1, d_a), lambda b, h, i: (0, 0))
    any_spec = pl.BlockSpec(memory_space=pl.ANY)
    blank = jnp.zeros(o_shape.shape, o_shape.dtype)
    oa = pl.pallas_call(
        functools.partial(_diff_prompt_kernel, t=t, d_a=d_a, lam_init=lam_init),
        grid=grid,
        in_specs=[q_spec(0), kv_spec, kv_spec,
                  pl.BlockSpec((None, 3, t, t), lambda b, h, i: (h, 0, 0, 0)),
                  vec, vec, vec, vec,
                  pl.BlockSpec((1, w2), lambda b, h, i: (0, 0)),
                  any_spec],
        out_specs=o_spec, out_shape=o_shape,
        input_output_aliases={9: 0},
        compiler_params=_params("parallel", "parallel", "parallel"),
        name="diff_attn_prompt",
    )(q, kv[0], kv[1], bias_tiles, *lam_vecs, subln_g, blank)
    assert d_b == w2
    ob = pl.pallas_call(
        functools.partial(_sb_prompt_kernel, t=t, d_b=d_b),
        grid=grid,
        in_specs=[q_spec(heads), kv_spec, kv_spec, any_spec],
        out_specs=o_spec, out_shape=o_shape,
        input_output_aliases={3: 0},
        compiler_params=_params("parallel", "parallel", "parallel"),
        name="sb_attn_prompt",
    )(q, kv[2], kv[3], blank)
    return oa, ob


def _sample_attention(q, kv, caches, layer, bias_c, bias_n, lam_vecs, subln_g, oa, ob, n_p, db, ts,
                      heads, d_a, d_b, lam_init):
    w2 = 2 * d_a
    wid = heads * w2
    rows = caches[0].shape[2]
    past = rows // heads
    assert n_p % ts == 0
    row0 = n_p // ts
    grid = (db,)
    q_spec = lambda off: pl.BlockSpec((ts, wid), lambda b: (row0 + b, off))
    n_spec = pl.BlockSpec((None, ts, wid), lambda b: (layer, b, 0))
    c_spec = pl.BlockSpec((None, None, rows, w2), lambda b: (layer, b, 0, 0))
    o_spec = pl.BlockSpec((ts, wid), lambda b: (row0 + b, 0))
    any_spec = pl.BlockSpec(memory_space=pl.ANY)
    vec = pl.BlockSpec((1, d_a), lambda b: (0, 0))
    oa = pl.pallas_call(
        functools.partial(_diff_sample_kernel, t=ts, d_a=d_a, lam_init=lam_init, heads=heads),
        grid=grid,
        in_specs=[q_spec(0), c_spec, c_spec, n_spec, n_spec,
                  pl.BlockSpec((heads, ts, past), lambda b: (0, 0, 0)),
                  pl.BlockSpec((heads, ts, ts), lambda b: (0, 0, 0)),
                  vec, vec, vec, vec,
                  pl.BlockSpec((1, w2), lambda b: (0, 0)),
                  any_spec],
        out_specs=o_spec, out_shape=jax.ShapeDtypeStruct(oa.shape, oa.dtype),
        input_output_aliases={12: 0},
        compiler_params=_params("parallel"),
        name="diff_attn_sample",
    )(q, caches[0], caches[1], kv[0], kv[1], bias_c, bias_n, *lam_vecs, subln_g, oa)
    tc = _tile(past, 1024)
    ob = pl.pallas_call(
        functools.partial(_sb_sample_kernel, t=ts, d_b=d_b, tc=tc, heads=heads),
        grid=grid,
        in_specs=[q_spec(1), c_spec, c_spec, n_spec, n_spec, any_spec],
        out_specs=o_spec, out_shape=jax.ShapeDtypeStruct(ob.shape, ob.dtype),
        input_output_aliases={5: 0},
        compiler_params=_params("parallel"),
        name="sb_attn_sample",
    )(q, caches[2], caches[3], kv[2], kv[3], ob)
    return oa, ob


def _merge_kernel(oa_ref, ob_ref, g_ref, wa_ref, wb_ref, o_ref, *, d):
    a = jnp.dot(oa_ref[...], wa_ref[...], preferred_element_type=F32)
    b = jnp.dot(ob_ref[...], wb_ref[...], preferred_element_type=F32)
    ga = jax.nn.sigmoid(g_ref[:, :d])
    gb = jax.nn.sigmoid(g_ref[:, d:])
    o_ref[...] = (ga * a + gb * b).astype(o_ref.dtype)


def _merge(oa, ob, gates, wa, wb):
    n, w = oa.shape
    d = wa.shape[1]
    tm = _tile(n, TOKEN_TILE // 2)
    return pl.pallas_call(
        functools.partial(_merge_kernel, d=d),
        grid=(n // tm,),
        in_specs=[pl.BlockSpec((tm, w), lambda i: (i, 0)),
                  pl.BlockSpec((tm, w), lambda i: (i, 0)),
                  pl.BlockSpec((tm, 2 * d), lambda i: (i, 0)),
                  pl.BlockSpec((w, d), lambda i: (0, 0)),
                  pl.BlockSpec((w, d), lambda i: (0, 0))],
        out_specs=pl.BlockSpec((tm, d), lambda i: (i, 0)),
        out_shape=jax.ShapeDtypeStruct((n, d), BF16),
        compiler_params=_params("parallel"),
        name="branch_merge",
    )(oa, ob, gates, wa, wb)


def _out_proj_kernel(m_ref, w_ref, r_ref, g_ref, x_ref, *h_refs):
    x = jnp.dot(m_ref[...], w_ref[...], preferred_element_type=F32) + r_ref[...]
    x_ref[...] = x
    y = _rms(x, g_ref[...])
    for h_ref in h_refs:
        h_ref[...] = y.astype(h_ref.dtype)


def _out_proj(merged, w, res, g, norm_dtypes):
    n, d = res.shape
    k = merged.shape[1]
    tm = _tile(n, TOKEN_TILE // 2)
    row = lambda c: pl.BlockSpec((tm, c), lambda i: (i, 0))
    return pl.pallas_call(
        _out_proj_kernel,
        grid=(n // tm,),
        in_specs=[row(k), pl.BlockSpec((k, d), lambda i: (0, 0)), row(d),
                  pl.BlockSpec((1, d), lambda i: (0, 0))],
        out_specs=[row(d)] + [row(d) for _ in norm_dtypes],
        out_shape=[jax.ShapeDtypeStruct((n, d), F32)]
                  + [jax.ShapeDtypeStruct((n, d), dt) for dt in norm_dtypes],
        compiler_params=_params("parallel"),
        name="proj_out",
    )(merged, w, res, g.reshape(1, d).astype(F32))


def _ffn_kernel(te_ref, tv_ref, x_ref, wg_ref, wu_ref, wd_ref, *rest, has_res):
    if has_res:
        res_ref, o_ref, xb_ref, acc_ref = rest
    else:
        res_ref = None
        o_ref, xb_ref, acc_ref = rest
    i = pl.program_id(0)
    j = pl.program_id(1)
    last = pl.num_programs(1) - 1
    valid = tv_ref[i] > 0

    @pl.when(jnp.logical_and(valid, j == 0))
    def _():
        xb_ref[...] = x_ref[...].astype(BF16)
        acc_ref[...] = jnp.zeros_like(acc_ref)

    @pl.when(valid)
    def _():
        x = xb_ref[...]
        g = jnp.dot(x, wg_ref[...], preferred_element_type=F32)
        u = jnp.dot(x, wu_ref[...], preferred_element_type=F32)
        a = (g * jax.nn.sigmoid(g)) * u
        acc_ref[...] += jnp.dot(a.astype(BF16), wd_ref[...], preferred_element_type=F32)

    @pl.when(jnp.logical_and(valid, j == last))
    def _():
        out = acc_ref[...]
        if has_res:
            out = out + res_ref[...]
        o_ref[...] = out

    @pl.when(jnp.logical_and(jnp.logical_not(valid), j == last))
    def _():
        o_ref[...] = jnp.zeros_like(o_ref)


def _ffn(x, wg, wu, wd, tile_expert, tile_valid, tm, name, res=None):
    n, d = x.shape
    f = wg.shape[2]
    tf = _tile(f, FF_TILE)
    nj = f // tf

    def fj(i, j, tv):
        return jnp.where(tv[i] > 0, j, nj - 1)

    in_specs = [pl.BlockSpec((tm, d), lambda i, j, te, tv: (i, 0)),
                pl.BlockSpec((None, d, tf), lambda i, j, te, tv: (te[i], 0, fj(i, j, tv))),
                pl.BlockSpec((None, d, tf), lambda i, j, te, tv: (te[i], 0, fj(i, j, tv))),
                pl.BlockSpec((None, tf, d), lambda i, j, te, tv: (te[i], fj(i, j, tv), 0))]
    args = [x, wg, wu, wd]
    if res is not None:
        in_specs.append(pl.BlockSpec((tm, d), lambda i, j, te, tv: (i, 0)))
        args.append(res)
    return pl.pallas_call(
        functools.partial(_ffn_kernel, has_res=res is not None),
        grid_spec=pltpu.PrefetchScalarGridSpec(
            num_scalar_prefetch=2,
            grid=(n // tm, nj),
            in_specs=in_specs,
            out_specs=pl.BlockSpec((tm, d), lambda i, j, te, tv: (i, 0)),
            scratch_shapes=[pltpu.VMEM((tm, d), BF16), pltpu.VMEM((tm, d), F32)]),
        out_shape=jax.ShapeDtypeStruct((n, d), F32),
        compiler_params=_params("parallel", "arbitrary"),
        name=name,
    )(tile_expert, tile_valid, *args)


def _router_kernel(h_ref, w_ref, b_ref, idx_ref, gate_ref, cnt_ref, run_ref, *, tm):
    i = pl.program_id(0)

    @pl.when(i == 0)
    def _():
        run_ref[...] = jnp.zeros_like(run_ref)

    logits = jnp.dot(h_ref[...], w_ref[...], preferred_element_type=F32) + b_ref[...]
    lane = lax.broadcasted_iota(jnp.int32, logits.shape, 1)
    m1 = jnp.max(logits, axis=1, keepdims=True)
    i1 = jnp.min(jnp.where(logits == m1, lane, LANES), axis=1, keepdims=True)
    oh1 = lane == i1
    rest = jnp.where(oh1, -jnp.inf, logits)
    m2 = jnp.max(rest, axis=1, keepdims=True)
    i2 = jnp.min(jnp.where(rest == m2, lane, LANES), axis=1, keepdims=True)
    oh2 = lane == i2
    e = jnp.exp(m2 - m1)
    g1 = 1.0 / (1.0 + e)
    g2 = e / (1.0 + e)

    row = lax.broadcasted_iota(jnp.int32, (tm, tm), 0)
    col = lax.broadcasted_iota(jnp.int32, (tm, tm), 1)
    tri = (col < row).astype(BF16)
    picked = jnp.logical_or(oh1, oh2).astype(BF16)
    before = jnp.dot(tri, picked, preferred_element_type=F32) + run_ref[...]
    r1 = jnp.sum(jnp.where(oh1, before, 0.0), axis=1, keepdims=True).astype(jnp.int32)
    r2 = jnp.sum(jnp.where(oh2, before, 0.0), axis=1, keepdims=True).astype(jnp.int32)
    run_ref[...] += jnp.sum(picked.astype(F32), axis=0, keepdims=True)

    idx_ref[...] = jnp.where(lane == 0, i1, jnp.where(lane == 1, i2,
                             jnp.where(lane == 2, r1, jnp.where(lane == 3, r2, 0))))
    gate_ref[...] = jnp.where(lane == 0, g1, jnp.where(lane == 1, g2, 0.0))
    cnt_ref[...] = run_ref[...]


def _router(h, w_r, b_r):
    n, d = h.shape
    ne = w_r.shape[1]
    tm = _tile(n, TOKEN_TILE)
    w_pad = jnp.zeros((d, LANES), BF16).at[:, :ne].set(w_r.astype(BF16))
    b_pad = jnp.full((1, LANES), NEG_INF, F32).at[0, :ne].set(b_r.astype(F32))
    return pl.pallas_call(
        functools.partial(_router_kernel, tm=tm),
        grid=(n // tm,),
        in_specs=[pl.BlockSpec((tm, d), lambda i: (i, 0)),
                  pl.BlockSpec((d, LANES), lambda i: (0, 0)),
                  pl.BlockSpec((1, LANES), lambda i: (0, 0))],
        out_specs=[pl.BlockSpec((tm, LANES), lambda i: (i, 0)),
                   pl.BlockSpec((tm, LANES), lambda i: (i, 0)),
                   pl.BlockSpec((1, LANES), lambda i: (0, 0))],
        out_shape=[jax.ShapeDtypeStruct((n, LANES), jnp.int32),
                   jax.ShapeDtypeStruct((n, LANES), F32),
                   jax.ShapeDtypeStruct((1, LANES), F32)],
        scratch_shapes=[pltpu.VMEM((1, LANES), F32)],
        compiler_params=_params("arbitrary"),
        name="moe_router",
    )(h, w_pad, b_pad)


def _row_copy(src, src_row, dst, dst_row, sem):
    return pltpu.make_async_copy(src.at[pl.ds(src_row, 1), :], dst.at[pl.ds(dst_row, 1), :], sem)


def _dispatch_kernel(src_ref, h_ref, o_ref, buf_ref, sem, *, tt):
    def issue(r, c):
        _row_copy(h_ref, src_ref[0, 0, r], buf_ref, r, sem).start()
        return c

    lax.fori_loop(0, tt, issue, 0, unroll=8)

    def drain(r, c):
        _row_copy(h_ref, 0, buf_ref, 0, sem).wait()
        return c

    lax.fori_loop(0, tt, drain, 0, unroll=8)
    o_ref[...] = buf_ref[...].astype(o_ref.dtype)


def _dispatch(h, src):
    d = h.shape[1]
    n_slots = src.shape[0]
    tt = _tile(n_slots, ROW_DMA_BYTES // (d * h.dtype.itemsize))
    return pl.pallas_call(
        functools.partial(_dispatch_kernel, tt=tt),
        grid=(n_slots // tt,),
        in_specs=[pl.BlockSpec((1, 1, tt), lambda i: (i, 0, 0), memory_space=pltpu.SMEM),
                  pl.BlockSpec(memory_space=pl.ANY)],
        out_specs=pl.BlockSpec((tt, d), lambda i: (i, 0)),
        out_shape=jax.ShapeDtypeStruct((n_slots, d), BF16),
        scratch_shapes=[pltpu.VMEM((tt, d), h.dtype), pltpu.SemaphoreType.DMA(())],
        compiler_params=_params("arbitrary"),
        name="moe_dispatch",
    )(src.reshape(n_slots // tt, 1, tt), h)


def _combine_kernel(pos_ref, x_ref, gate_ref, y_ref, g_ref, *rest, tt, want_x):
    if want_x:
        x_out_ref, o_ref, buf_ref, sem = rest
    else:
        o_ref, buf_ref, sem = rest

    def issue(r, c):
        for k in range(TOP_K):
            _row_copy(y_ref, pos_ref[0, 0, TOP_K * r + k], buf_ref.at[k], r, sem).start()
        return c

    lax.fori_loop(0, tt, issue, 0, unroll=8)

    def drain(r, c):
        for k in range(TOP_K):
            _row_copy(y_ref, 0, buf_ref.at[k], 0, sem).wait()
        return c

    lax.fori_loop(0, tt, drain, 0, unroll=8)
    g = gate_ref[...]
    x = x_ref[...] + (g[:, 0:1] * buf_ref[0] + g[:, 1:2] * buf_ref[1])
    if want_x:
        x_out_ref[...] = x
    o_ref[...] = _rms(x, g_ref[...]).astype(o_ref.dtype)


def _combine_norm(x, gates, y, pos, g, row0, n_rows, norm_dtype, want_x):
    n, d = x.shape
    tt = _tile(n_rows, ROW_DMA_BYTES // (TOP_K * d * y.dtype.itemsize))
    assert n % tt == 0 and row0 % tt == 0
    off = row0 // tt
    pos3 = pos.reshape(n // tt, 1, TOP_K * tt)
    out = pl.BlockSpec((tt, d), lambda i: (i, 0))
    return pl.pallas_call(
        functools.partial(_combine_kernel, tt=tt, want_x=want_x),
        grid=(n_rows // tt,),
        in_specs=[pl.BlockSpec((1, 1, TOP_K * tt), lambda i: (off + i, 0, 0), memory_space=pltpu.SMEM),
                  pl.BlockSpec((tt, d), lambda i: (off + i, 0)),
                  pl.BlockSpec((tt, LANES), lambda i: (off + i, 0)),
                  pl.BlockSpec(memory_space=pl.ANY),
                  pl.BlockSpec((1, d), lambda i: (0, 0))],
        out_specs=[out] * (2 if want_x else 1),
        out_shape=([jax.ShapeDtypeStruct((n_rows, d), F32)] if want_x else [])
                  + [jax.ShapeDtypeStruct((n_rows, d), norm_dtype)],
        scratch_shapes=[pltpu.VMEM((TOP_K, tt, d), F32), pltpu.SemaphoreType.DMA(())],
        compiler_params=_params("arbitrary"),
        name="moe_combine",
    )(pos3, x, gates, y, g.reshape(1, d).astype(F32))


def _moe_route(h_bf, w_r, b_r, ne, tm):
    n = h_bf.shape[0]
    idx, gates, counts = _router(h_bf, w_r, b_r)
    counts = counts[0, :ne].astype(jnp.int32)
    padded = ((counts + tm - 1) // tm) * tm
    ends = jnp.cumsum(padded)
    offsets = ends - padded
    expert = idx[:, :TOP_K]
    pos = jnp.zeros_like(expert)
    for e in range(ne):
        pos = jnp.where(expert == e, offsets[e], pos)
    pos = pos + idx[:, TOP_K:2 * TOP_K]
    n_tiles = (n * TOP_K) // tm + ne
    tile_start = jnp.arange(n_tiles, dtype=jnp.int32) * tm
    tile_valid = (tile_start < ends[-1]).astype(jnp.int32)
    tile_expert = jnp.minimum(jnp.sum((tile_start[:, None] >= ends[None, :]).astype(jnp.int32), axis=1),
                              ne - 1)
    last_expert = jnp.max(jnp.where(counts > 0, jnp.arange(ne, dtype=jnp.int32), 0))
    tile_expert = jnp.where(tile_valid > 0, tile_expert, last_expert)
    token = jnp.broadcast_to(jnp.arange(n, dtype=jnp.int32)[:, None], pos.shape)
    src = jnp.zeros((n_tiles * tm,), jnp.int32).at[pos.reshape(-1)].set(
        token.reshape(-1), unique_indices=True)
    return pos, src, gates, tile_expert, tile_valid


def kernel(x_prompt, x_sample, cache_diff_k, cache_diff_v, cache_sb_k, cache_sb_v, rel_bias, norm_mix_g, w_in, lambda_q1, lambda_k1, lambda_q2, lambda_k2, subln_g, w_branch_a, w_branch_b, w_out, norm_ffn_g, dense_w_gate, dense_w_up, dense_w_down, router_w, router_b, moe_w_gate, moe_w_up, moe_w_down, final_norm_g):
    batch, seq, d = x_prompt.shape
    db, ts, _ = x_sample.shape
    depth, _, past, heads, w2 = cache_diff_k.shape
    d_a = w2 // 2
    d_b = cache_sb_k.shape[-1]
    wid = heads * w2
    n_p = batch * seq
    n_s = db * ts
    n_tok = n_p + n_s
    assert d % wid == 0 and w_in.shape[2] == 6 * wid + 2 * d
    assert cache_sb_k.shape[3] * d_b == wid and cache_diff_v.shape[-1] == w2

    x, h = _join_norm(x_prompt.reshape(n_p, d), x_sample.reshape(n_s, d), norm_mix_g[0])

    t_attn = _tile(seq, ATTN_TILE)
    pos_q = past + jnp.arange(ts, dtype=jnp.int32)
    bias_tiles = _prompt_bias_tiles(rel_bias, t_attn)
    bias_c = _masked_bias(rel_bias, pos_q, jnp.arange(past, dtype=jnp.int32))
    bias_n = _masked_bias(rel_bias, pos_q, pos_q)

    caches = [c.reshape(depth, db, past * heads, w2)
              for c in (cache_diff_k, cache_diff_v, cache_sb_k, cache_sb_v)]
    cache_p = [None] * 4
    cache_s = [None] * 4
    kv_cols = (1, 2, 4, 5)
    for l in range(depth):
        lam_init = 0.8 - 0.6 * math.exp(-0.3 * l)
        (q,) = _proj(h, w_in, l, lambda j: 3 * j, 2, wid,
                     [((n_tok, 2 * wid), BF16, (_tile(n_tok, TOKEN_TILE), wid), lambda j, i: (i, j))],
                     "proj_q")
        (gates,) = _proj(h, w_in, l, lambda j: j + 6, 2 * d // wid, wid,
                         [((n_tok, 2 * d), F32, (_tile(n_tok, TOKEN_TILE), wid), lambda j, i: (i, j))],
                         "proj_gate")
        for c in range(4):
            cache_p[c] = _proj_cache_rows(h, w_in, l, kv_cols[c], wid, 0, n_p, cache_p[c], "proj_kv_prompt")
            cache_s[c] = _proj_cache_rows(h, w_in, l, kv_cols[c], wid, n_p, n_s, cache_s[c], "proj_kv_sample")

        lam_vecs = [v[l].reshape(1, d_a).astype(F32) for v in (lambda_q1, lambda_k1, lambda_q2, lambda_k2)]
        g_sub = subln_g[l].reshape(1, w2).astype(F32)
        oa, ob = _prompt_attention(q, cache_p, l, bias_tiles, lam_vecs, g_sub, n_tok, batch, seq, heads, d_a,
                                   d_b, t_attn, lam_init)
        oa, ob = _sample_attention(q, cache_s, caches, l, bias_c, bias_n, lam_vecs, g_sub, oa, ob, n_p, db,
                                   ts, heads, d_a, d_b, lam_init)
        merged = _merge(oa, ob, gates, w_branch_a[l].astype(BF16), w_branch_b[l].astype(BF16))

        j = l // 2
        last = l + 1 == depth
        tm = _tile(n_tok, TOKEN_TILE)
        if l % 2 == 0:
            x, h2 = _out_proj(merged, w_out[l].astype(BF16), x, norm_ffn_g[l], [BF16])
            n_tiles = n_tok // tm
            x = _ffn(h2, dense_w_gate[j:j + 1].astype(BF16), dense_w_up[j:j + 1].astype(BF16),
                     dense_w_down[j:j + 1].astype(BF16), jnp.zeros((n_tiles,), jnp.int32),
                     jnp.ones((n_tiles,), jnp.int32), tm, "dense_ffn", res=x)
            if last:
                (y_prompt,) = _rmsnorm(x, final_norm_g, [F32], "norm_final", 0, n_p)
                (y_sample,) = _rmsnorm(x, final_norm_g, [F32], "norm_final", n_p, n_s)
            else:
                (h,) = _rmsnorm(x, norm_mix_g[l + 1], [BF16], "norm_mix")
        else:
            x, h2, h2_f32 = _out_proj(merged, w_out[l].astype(BF16), x, norm_ffn_g[l], [BF16, F32])
            ne = moe_w_gate.shape[1]
            pos, src, route_g, tile_expert, tile_valid = _moe_route(h2, router_w[j], router_b[j], ne, tm)
            xs = _dispatch(h2_f32, src)
            y = _ffn(xs, moe_w_gate[j].astype(BF16), moe_w_up[j].astype(BF16), moe_w_down[j].astype(BF16),
                     tile_expert, tile_valid, tm, "moe_ffn")
            if last:
                (y_prompt,) = _combine_norm(x, route_g, y, pos, final_norm_g, 0, n_p, F32, False)
                (y_sample,) = _combine_norm(x, route_g, y, pos, final_norm_g, n_p, n_s, F32, False)
            else:
                x, h = _combine_norm(x, route_g, y, pos, norm_mix_g[l + 1], 0, n_tok, BF16, True)

    outs_p = [c.reshape(depth, batch, seq, heads, w2) for c in cache_p]
    outs_s = [c.reshape(depth, db, ts, heads, w2) for c in cache_s]
    return (y_prompt.reshape(batch, seq, d), y_sample.reshape(db, ts, d), *outs_p, *outs_s)
```

```python
import functools
import math

import jax
import jax.numpy as jnp
from jax import lax
from jax.experimental import pallas as pl
from jax.experimental.pallas import tpu as pltpu

CHUNK = 64
NUM_BUCKETS = 32
MAX_DISTANCE = 128
TOP_K = 2
EPS = 1e-6
NEG_INF = -1e30

LANES = 128
VMEM_LIMIT_BYTES = 56 * 1024 * 1024
TOKEN_TILE = 512
ATTN_TILE = 512
FF_TILE = 512
MOE_ROW_TILE = 1024
MOE_FF_TILE = 256
ROW_DMA_BYTES = 4 * 1024 * 1024

BF16 = jnp.bfloat16
F32 = jnp.float32


def _params(*sem):
    return pltpu.CompilerParams(dimension_semantics=sem, vmem_limit_bytes=VMEM_LIMIT_BYTES)


def _tile(n, want):
    t = min(n, want)
    while n % t:
        t //= 2
    return t


def _rms(x, g):
    return x * lax.rsqrt(jnp.mean(x * x, axis=-1, keepdims=True) + EPS) * g


def _rmsnorm_kernel(x_ref, g_ref, *o_refs):
    y = _rms(x_ref[...].astype(F32), g_ref[...])
    for o_ref in o_refs:
        o_ref[...] = y.astype(o_ref.dtype)


def _rmsnorm(x, g, out_dtypes, name, row0=0, n_rows=None):
    d = x.shape[1]
    n_rows = x.shape[0] if n_rows is None else n_rows
    tm = _tile(n_rows, TOKEN_TILE)
    assert row0 % tm == 0
    off = row0 // tm
    return pl.pallas_call(
        _rmsnorm_kernel,
        grid=(n_rows // tm,),
        in_specs=[pl.BlockSpec((tm, d), lambda i: (off + i, 0)),
                  pl.BlockSpec((1, d), lambda i: (0, 0))],
        out_specs=[pl.BlockSpec((tm, d), lambda i: (i, 0)) for _ in out_dtypes],
        out_shape=[jax.ShapeDtypeStruct((n_rows, d), dt) for dt in out_dtypes],
        compiler_params=_params("parallel"),
        name=name,
    )(x, g.reshape(1, d).astype(F32))


def _join_norm_kernel(xp_ref, xs_ref, g_ref, x_ref, h_ref, *, np_tiles):
    x = jnp.where(pl.program_id(0) < np_tiles, xp_ref[...], xs_ref[...])
    x_ref[...] = x
    h_ref[...] = _rms(x, g_ref[...]).astype(h_ref.dtype)


def _join_norm(xp, xs, g):
    (n_p, d), n_s = xp.shape, xs.shape[0]
    tm = _tile(n_s, TOKEN_TILE)
    assert n_p % tm == 0
    np_tiles = n_p // tm
    out = pl.BlockSpec((tm, d), lambda i: (i, 0))
    return pl.pallas_call(
        functools.partial(_join_norm_kernel, np_tiles=np_tiles),
        grid=((n_p + n_s) // tm,),
        in_specs=[pl.BlockSpec((tm, d), lambda i: (jnp.minimum(i, np_tiles - 1), 0)),
                  pl.BlockSpec((tm, d), lambda i: (jnp.maximum(i - np_tiles, 0), 0)),
                  pl.BlockSpec((1, d), lambda i: (0, 0))],
        out_specs=[out, out],
        out_shape=[jax.ShapeDtypeStruct((n_p + n_s, d), F32), jax.ShapeDtypeStruct((n_p + n_s, d), BF16)],
        compiler_params=_params("parallel"),
        name="join_norm",
    )(xp, xs, g.reshape(1, d).astype(F32))


def _proj_kernel(x_ref, w_ref, *rest, n_out):
    o_refs, wb_ref = rest[-1 - n_out:-1], rest[-1]

    @pl.when(pl.program_id(1) == 0)
    def _():
        wb_ref[...] = w_ref[...].astype(BF16)

    acc = jnp.dot(x_ref[...], wb_ref[...], preferred_element_type=F32)
    for o_ref in o_refs:
        o_ref[...] = acc.astype(o_ref.dtype)


def _proj(x, w, layer, col_block_fn, n_blocks, tn, outs, name, row0=0, n_rows=None):
    k = x.shape[1]
    n_rows = x.shape[0] if n_rows is None else n_rows
    tm = _tile(n_rows, TOKEN_TILE)
    assert row0 % tm == 0
    off = row0 // tm
    return pl.pallas_call(
        functools.partial(_proj_kernel, n_out=len(outs)),
        grid=(n_blocks, n_rows // tm),
        in_specs=[pl.BlockSpec((tm, k), lambda j, i: (off + i, 0)),
                  pl.BlockSpec((None, k, tn), lambda j, i: (layer, 0, col_block_fn(j)))],
        out_specs=[pl.BlockSpec(o[2], o[3]) for o in outs],
        out_shape=[jax.ShapeDtypeStruct(o[0], o[1]) for o in outs],
        scratch_shapes=[pltpu.VMEM((k, tn), BF16)],
        compiler_params=_params("parallel", "arbitrary"),
        name=name,
    )(x, w)


def _cache_rows_kernel(x_ref, w_ref, *rest):
    o_ref, wb_ref = rest[-2:]
    s = pl.program_id(0)

    @pl.when(jnp.logical_and(s == 0, pl.program_id(1) == 0))
    def _():
        wb_ref[...] = w_ref[...].astype(BF16)

    @pl.when(s == 0)
    def _():
        o_ref[...] = jnp.dot(x_ref[...], wb_ref[...], preferred_element_type=F32)

    @pl.when(s > 0)
    def _():
        o_ref[...] = jnp.zeros_like(o_ref)


def _proj_cache_rows(h, w_in, layer, col, wid, row0, n_rows, prev, name):
    depth, k, _ = w_in.shape
    tm = _tile(n_rows, TOKEN_TILE)
    assert row0 % tm == 0
    off = row0 // tm
    nt = n_rows // tm
    tile = lambda s, i: jnp.where(s == 0, i, nt - 1)
    in_specs = [pl.BlockSpec((tm, k), lambda s, i: (off + tile(s, i), 0)),
                pl.BlockSpec((None, k, wid), lambda s, i: (layer, 0, col))]
    args = [h, w_in]
    if prev is not None:
        in_specs.append(pl.BlockSpec(memory_space=pl.ANY))
        args.append(prev)
    return pl.pallas_call(
        _cache_rows_kernel,
        grid=(1 if prev is not None else depth, nt),
        in_specs=in_specs,
        out_specs=pl.BlockSpec((None, tm, wid), lambda s, i: ((layer + s) % depth, i, 0)),
        out_shape=jax.ShapeDtypeStruct((depth, n_rows, wid), F32),
        scratch_shapes=[pltpu.VMEM((k, wid), BF16)],
        input_output_aliases={2: 0} if prev is not None else {},
        compiler_params=_params("arbitrary", "arbitrary"),
        name=name,
    )(*args)


def _t5_bucket(rel):
    half = NUM_BUCKETS // 2
    max_exact = half // 2
    n = jnp.abs(rel)
    far = max_exact + (jnp.log(jnp.maximum(n, 1).astype(F32) / max_exact)
                       / math.log(MAX_DISTANCE / max_exact) * (half - max_exact)).astype(jnp.int32)
    far = jnp.minimum(far, half - 1)
    return jnp.where(rel > 0, half, 0) + jnp.where(n < max_exact, n, far)


def _masked_bias(rel_bias, q_pos, k_pos):
    bucket = _t5_bucket(k_pos[None, :] - q_pos[:, None])
    bias = jnp.zeros((rel_bias.shape[1],) + bucket.shape, F32)
    for b in range(NUM_BUCKETS):
        bias = jnp.where(bucket[None] == b, rel_bias[b].astype(F32)[:, None, None], bias)
    mask = (k_pos[None, :] // CHUNK) <= (q_pos[:, None] // CHUNK)
    return jnp.where(mask[None], bias, NEG_INF)


def _prompt_bias_tiles(rel_bias, t):
    assert t >= MAX_DISTANCE and t % CHUNK == 0
    pos = jnp.arange(t, dtype=jnp.int32)
    tiles = [_masked_bias(rel_bias, 2 * t + pos, (2 - d) * t + pos) for d in (2, 1, 0)]
    return jnp.stack(tiles, axis=1)


def _split_maps(q, d_a):
    lane = lax.broadcasted_iota(jnp.int32, q.shape, 1)
    zero = jnp.zeros_like(q)
    return jnp.concatenate([jnp.where(lane < d_a, q, zero), jnp.where(lane >= d_a, q, zero)], axis=0)


def _lambda(lq1, lk1, lq2, lk2, lam_init):
    return (jnp.exp(jnp.sum(lq1 * lk1, axis=-1, keepdims=True))
            - jnp.exp(jnp.sum(lq2 * lk2, axis=-1, keepdims=True)) + lam_init)


def _diff_finish(o, lam, g, t, lam_init):
    oa = o[:t] - lam * o[t:]
    return _rms(oa, g) * (1.0 - lam_init)


def _nt_dot(a, b):
    return lax.dot_general(a, b, (((1,), (1,)), ((), ())), preferred_element_type=F32)


def _diff_prompt_kernel(q_ref, k_ref, v_ref, bias_ref, lq1_ref, lk1_ref, lq2_ref, lk2_ref, g_ref,
                        o_in_ref, o_ref, *, t, d_a, lam_init):
    del o_in_ref
    qi = pl.program_id(2)
    scale = d_a ** -0.5
    q = q_ref[...]
    exact_scale = math.frexp(scale)[0] == 0.5
    if exact_scale:
        q = (q.astype(F32) * scale).astype(q.dtype)
    qq = _split_maps(q, d_a)
    dv = v_ref.shape[-1]
    half = t // 2
    assert half % CHUNK == 0

    def update(sc, bias, state, vb):
        m, l, acc = state
        sc = (sc if exact_scale else sc * scale) + bias
        m_new = jnp.maximum(m, jnp.max(sc, axis=1, keepdims=True))
        alpha = jnp.exp(m - m_new)
        p = jnp.exp(sc - m_new)
        l = alpha * l + jnp.sum(p, axis=1, keepdims=True)
        acc = alpha * acc + jnp.dot(p.astype(BF16), vb, preferred_element_type=F32)
        return m_new, l, acc

    def body(j, carry):
        start = pl.multiple_of(j * t, t)
        vb = v_ref[pl.ds(start, t), :].astype(BF16)
        bt = bias_ref[jnp.clip(j - qi + 2, 0, 1)]
        s = _nt_dot(qq, k_ref[pl.ds(start, t), :].astype(BF16))
        return tuple(update(s[c * t:(c + 1) * t], bt, carry[c], vb) for c in range(2))

    one = (jnp.full((t, 1), NEG_INF, F32), jnp.zeros((t, 1), F32), jnp.zeros((t, dv), F32))
    carry = lax.fori_loop(0, qi, body, (one, one))

    start = pl.multiple_of(qi * t, t)
    bt = bias_ref[2]
    rows = lambda x, c, r: x[c * t + r * half:c * t + (r + 1) * half]
    q_top = jnp.concatenate([rows(qq, 0, 0), rows(qq, 1, 0)], axis=0)
    q_bot = jnp.concatenate([rows(qq, 0, 1), rows(qq, 1, 1)], axis=0)
    s_top = _nt_dot(q_top, k_ref[pl.ds(start, half), :].astype(BF16))
    s_bot = _nt_dot(q_bot, k_ref[pl.ds(start, t), :].astype(BF16))
    outs = []
    for c in range(2):
        state = [tuple(x[r * half:(r + 1) * half] for x in carry[c]) for r in range(2)]
        top = update(s_top[c * half:(c + 1) * half], bt[:half, :half], state[0],
                     v_ref[pl.ds(start, half), :].astype(BF16))
        bot = update(s_bot[c * half:(c + 1) * half], bt[half:, :], state[1], v_ref[pl.ds(start, t), :].astype(BF16))
        outs.append(jnp.concatenate([top[2] / top[1], bot[2] / bot[1]], axis=0))
    lam = _lambda(lq1_ref[...], lk1_ref[...], lq2_ref[...], lk2_ref[...], lam_init)
    o = jnp.concatenate(outs, axis=0)
    o_ref[...] = _diff_finish(o, lam, g_ref[...], t, lam_init).astype(o_ref.dtype)


def _head_rows(ref, h, heads, start, size):
    return ref[pl.ds(start * heads + h, size, stride=heads), :]


def _diff_sample_kernel(q_ref, kc_ref, vc_ref, kn_ref, vn_ref, bc_ref, bn_ref, lq1_ref, lk1_ref,
                        lq2_ref, lk2_ref, g_ref, o_in_ref, o_ref, *, t, d_a, lam_init, heads):
    del o_in_ref
    scale = d_a ** -0.5
    w2 = 2 * d_a
    past = kc_ref.shape[0] // heads
    lam = _lambda(lq1_ref[...], lk1_ref[...], lq2_ref[...], lk2_ref[...], lam_init)
    for h in range(heads):
        cols = slice(h * w2, (h + 1) * w2)
        qq = _split_maps(q_ref[:, cols], d_a)
        bc = bc_ref[h]
        bn = bn_ref[h]
        kc = _head_rows(kc_ref, h, heads, 0, past).astype(BF16)
        vc = _head_rows(vc_ref, h, heads, 0, past).astype(BF16)
        s_c = _nt_dot(qq, kc) * scale + jnp.concatenate([bc, bc], axis=0)
        s_n = _nt_dot(qq, kn_ref[:, cols].astype(BF16)) * scale + jnp.concatenate([bn, bn], axis=0)
        m = jnp.maximum(jnp.max(s_c, axis=1, keepdims=True), jnp.max(s_n, axis=1, keepdims=True))
        p_c = jnp.exp(s_c - m)
        p_n = jnp.exp(s_n - m)
        l = jnp.sum(p_c, axis=1, keepdims=True) + jnp.sum(p_n, axis=1, keepdims=True)
        acc = (jnp.dot(p_c.astype(BF16), vc, preferred_element_type=F32)
               + jnp.dot(p_n.astype(BF16), vn_ref[:, cols].astype(BF16), preferred_element_type=F32))
        o_ref[:, cols] = _diff_finish(acc / l, lam, g_ref[...], t, lam_init).astype(o_ref.dtype)


def _triangle(n):
    row = lax.broadcasted_iota(jnp.int32, (n, n), 0)
    col = lax.broadcasted_iota(jnp.int32, (n, n), 1)
    return (row > col).astype(BF16)


def _suffix_sums(x, tri):
    hi = x.astype(BF16)
    lo = (x - hi.astype(F32)).astype(BF16)
    return jnp.dot(hi, tri, preferred_element_type=F32) + jnp.dot(lo, tri, preferred_element_type=F32)


def _sb_span(q, kb, vb, r_sum, acc, scale, causal, n_parts):
    tk = kb.shape[0]
    part = tk // n_parts
    z = _nt_dot(q, kb) * scale
    sp = jnp.log(1.0 + jnp.exp(-jnp.abs(z)))
    ls = jnp.minimum(z, 0.0) - sp
    l1m = ls - z
    if causal is not None:
        l1m = jnp.where(causal, l1m, 0.0)
    tri = _triangle(part)
    between = [None] * n_parts
    for n in reversed(range(n_parts)):
        x = l1m[:, n * part:(n + 1) * part]
        between[n] = _suffix_sums(x, tri) + r_sum
        r_sum = r_sum + jnp.sum(x, axis=1, keepdims=True)
    a = jnp.exp(ls + jnp.concatenate(between, axis=1))
    if causal is not None:
        a = jnp.where(causal, a, 0.0)
    acc = acc + jnp.dot(a.astype(BF16), vb, preferred_element_type=F32)
    return r_sum, acc


def _strict_causal(tq, tk):
    row = lax.broadcasted_iota(jnp.int32, (tq, tk), 0)
    col = lax.broadcasted_iota(jnp.int32, (tq, tk), 1)
    return col < row


def _sb_parts(t):
    return max(1, t // 256)


def _sb_prompt_kernel(q_ref, k_ref, v_ref, o_in_ref, o_ref, *, t, d_b):
    del o_in_ref
    qi = pl.program_id(2)
    q = q_ref[...]
    scale = d_b ** -0.5
    dv = v_ref.shape[-1]
    parts = _sb_parts(t)
    start = pl.multiple_of(qi * t, t)
    half = t // 2
    zeros = (jnp.zeros((half, 1), F32), jnp.zeros((half, dv), F32))
    top = _sb_span(q[:half], k_ref[pl.ds(start, half), :].astype(BF16), v_ref[pl.ds(start, half), :].astype(BF16), *zeros,
                   scale, _strict_causal(half, half), _sb_parts(half))
    row = lax.broadcasted_iota(jnp.int32, (half, t), 0) + half
    col = lax.broadcasted_iota(jnp.int32, (half, t), 1)
    bottom = _sb_span(q[half:], k_ref[pl.ds(start, t), :].astype(BF16), v_ref[pl.ds(start, t), :].astype(BF16), *zeros,
                      scale, col < row, parts)
    carry = tuple(jnp.concatenate([a, b], axis=0) for a, b in zip(top, bottom))

    def body(n, carry):
        s = pl.multiple_of((qi - 1 - n) * t, t)
        return _sb_span(q, k_ref[pl.ds(s, t), :].astype(BF16), v_ref[pl.ds(s, t), :].astype(BF16),
                        carry[0], carry[1],
                        scale, None, parts)

    _, acc = lax.fori_loop(0, qi, body, carry)
    o_ref[...] = acc.astype(o_ref.dtype)


def _sb_sample_kernel(q_ref, kc_ref, vc_ref, kn_ref, vn_ref, o_in_ref, o_ref, *, t, d_b, tc, heads):
    del o_in_ref
    scale = d_b ** -0.5
    past = kc_ref.shape[0] // heads
    cols = [slice(h * d_b, (h + 1) * d_b) for h in range(heads)]
    qs = [q_ref[:, c] for c in cols]
    causal = _strict_causal(t, t)
    carry = tuple(_sb_span(qs[h], kn_ref[:, c].astype(BF16), vn_ref[:, c].astype(BF16),
                           jnp.zeros((t, 1), F32), jnp.zeros((t, d_b), F32), scale, causal, 1)
                  for h, c in enumerate(cols))

    def body(n, carry):
        s = pl.multiple_of(past - (n + 1) * tc, tc)
        return tuple(_sb_span(qs[h], _head_rows(kc_ref, h, heads, s, tc).astype(BF16),
                              _head_rows(vc_ref, h, heads, s, tc).astype(BF16), carry[h][0], carry[h][1],
                              scale, None, _sb_parts(tc))
                     for h in range(heads))

    carry = lax.fori_loop(0, past // tc, body, carry)
    for h, c in enumerate(cols):
        o_ref[:, c] = carry[h][1].astype(o_ref.dtype)


def _prompt_attention(q, kv, layer, bias_tiles, lam_vecs, subln_g, n_tok, batch, seq, heads, d_a, d_b, t,
                      lam_init):
    nq = seq // t
    grid = (batch, heads, nq)
    w2 = 2 * d_a
    q_spec = lambda off: pl.BlockSpec((t, w2), lambda b, h, i: (b * nq + i, off + h))
    kv_spec = pl.BlockSpec((None, seq, w2), lambda b, h, i: (layer, b, h))
    o_spec = pl.BlockSpec((t, w2), lambda b, h, i: (b * nq + i, h))
    o_shape = jax.ShapeDtypeStruct((n_tok, heads * w2), BF16)
    vec = pl.BlockSpec((1, d_a), lambda b, h, i: (0, 0))
    any_spec = pl.BlockSpec(memory_space=pl.ANY)
    blank = jnp.zeros(o_shape.shape, o_shape.dtype)
    oa = pl.pallas_call(
        functools.partial(_diff_prompt_kernel, t=t, d_a=d_a, lam_init=lam_init),
        grid=grid,
        in_specs=[q_spec(0), kv_spec, kv_spec,
                  pl.BlockSpec((None, 3, t, t), lambda b, h, i: (h, 0, 0, 0)),
                  vec, vec, vec, vec,
                  pl.BlockSpec((1, w2), lambda b, h, i: (0, 0)),
                  any_spec],
        out_specs=o_spec, out_shape=o_shape,
        input_output_aliases={9: 0},
        compiler_params=_params("parallel", "parallel", "parallel"),
        name="diff_attn_prompt",
    )(q, kv[0], kv[1], bias_tiles, *lam_vecs, subln_g, blank)
    assert d_b == w2
    ob = pl.pallas_call(
        functools.partial(_sb_prompt_kernel, t=t, d_b=d_b),
        grid=grid,
        in_specs=[q_spec(heads), kv_spec, kv_spec, any_spec],
        out_specs=o_spec, out_shape=o_shape,
        input_output_aliases={3: 0},
        compiler_params=_params("parallel", "parallel", "parallel"),
        name="sb_attn_prompt",
    )(q, kv[2], kv[3], blank)
    return oa, ob


def _sample_attention(q, kv, caches, layer, bias_c, bias_n, lam_vecs, subln_g, oa, ob, n_p, db, ts,
                      heads, d_a, d_b, lam_init):
    w2 = 2 * d_a
    wid = heads * w2
    rows = caches[0].shape[2]
    past = rows // heads
    assert n_p % ts == 0
    row0 = n_p // ts
    grid = (db,)
    q_spec = lambda off: pl.BlockSpec((ts, wid), lambda b: (row0 + b, off))
    n_spec = pl.BlockSpec((None, ts, wid), lambda b: (layer, b, 0))
    c_spec = pl.BlockSpec((None, None, rows, w2), lambda b: (layer, b, 0, 0))
    o_spec = pl.BlockSpec((ts, wid), lambda b: (row0 + b, 0))
    any_spec = pl.BlockSpec(memory_space=pl.ANY)
    vec = pl.BlockSpec((1, d_a), lambda b: (0, 0))
    oa = pl.pallas_call(
        functools.partial(_diff_sample_kernel, t=ts, d_a=d_a, lam_init=lam_init, heads=heads),
        grid=grid,
        in_specs=[q_spec(0), c_spec, c_spec, n_spec, n_spec,
                  pl.BlockSpec((heads, ts, past), lambda b: (0, 0, 0)),
                  pl.BlockSpec((heads, ts, ts), lambda b: (0, 0, 0)),
                  vec, vec, vec, vec,
                  pl.BlockSpec((1, w2), lambda b: (0, 0)),
                  any_spec],
        out_specs=o_spec, out_shape=jax.ShapeDtypeStruct(oa.shape, oa.dtype),
        input_output_aliases={12: 0},
        compiler_params=_params("parallel"),
        name="diff_attn_sample",
    )(q, caches[0], caches[1], kv[0], kv[1], bias_c, bias_n, *lam_vecs, subln_g, oa)
    tc = _tile(past, 1024)
    ob = pl.pallas_call(
        functools.partial(_sb_sample_kernel, t=ts, d_b=d_b, tc=tc, heads=heads),
        grid=grid,
        in_specs=[q_spec(1), c_spec, c_spec, n_spec, n_spec, any_spec],
        out_specs=o_spec, out_shape=jax.ShapeDtypeStruct(ob.shape, ob.dtype),
        input_output_aliases={5: 0},
        compiler_params=_params("parallel"),
        name="sb_attn_sample",
    )(q, caches[2], caches[3], kv[2], kv[3], ob)
    return oa, ob


def _merge_kernel(oa_ref, ob_ref, g_ref, wa_ref, wb_ref, o_ref, *, d):
    a = jnp.dot(oa_ref[...], wa_ref[...], preferred_element_type=F32)
    b = jnp.dot(ob_ref[...], wb_ref[...], preferred_element_type=F32)
    ga = jax.nn.sigmoid(g_ref[:, :d])
    gb = jax.nn.sigmoid(g_ref[:, d:])
    o_ref[...] = (ga * a + gb * b).astype(o_ref.dtype)


def _merge(oa, ob, gates, wa, wb):
    n, w = oa.shape
    d = wa.shape[1]
    tm = _tile(n, TOKEN_TILE // 2)
    return pl.pallas_call(
        functools.partial(_merge_kernel, d=d),
        grid=(n // tm,),
        in_specs=[pl.BlockSpec((tm, w), lambda i: (i, 0)),
                  pl.BlockSpec((tm, w), lambda i: (i, 0)),
                  pl.BlockSpec((tm, 2 * d), lambda i: (i, 0)),
                  pl.BlockSpec((w, d), lambda i: (0, 0)),
                  pl.BlockSpec((w, d), lambda i: (0, 0))],
        out_specs=pl.BlockSpec((tm, d), lambda i: (i, 0)),
        out_shape=jax.ShapeDtypeStruct((n, d), BF16),
        compiler_params=_params("parallel"),
        name="branch_merge",
    )(oa, ob, gates, wa, wb)


def _out_proj_kernel(m_ref, w_ref, r_ref, g_ref, x_ref, *h_refs):
    x = jnp.dot(m_ref[...], w_ref[...], preferred_element_type=F32) + r_ref[...]
    x_ref[...] = x
    y = _rms(x, g_ref[...])
    for h_ref in h_refs:
        h_ref[...] = y.astype(h_ref.dtype)


def _out_proj(merged, w, res, g, norm_dtypes):
    n, d = res.shape
    k = merged.shape[1]
    tm = _tile(n, TOKEN_TILE // 2)
    row = lambda c: pl.BlockSpec((tm, c), lambda i: (i, 0))
    return pl.pallas_call(
        _out_proj_kernel,
        grid=(n // tm,),
        in_specs=[row(k), pl.BlockSpec((k, d), lambda i: (0, 0)), row(d),
                  pl.BlockSpec((1, d), lambda i: (0, 0))],
        out_specs=[row(d)] + [row(d) for _ in norm_dtypes],
        out_shape=[jax.ShapeDtypeStruct((n, d), F32)]
                  + [jax.ShapeDtypeStruct((n, d), dt) for dt in norm_dtypes],
        compiler_params=_params("parallel"),
        name="proj_out",
    )(merged, w, res, g.reshape(1, d).astype(F32))


def _ffn_kernel(te_ref, tv_ref, x_ref, wg_ref, wu_ref, wd_ref, *rest, has_res):
    if has_res:
        res_ref, o_ref, acc_ref = rest
    else:
        res_ref = None
        o_ref, acc_ref = rest
    i = pl.program_id(0)
    j = pl.program_id(1)
    last = pl.num_programs(1) - 1
    valid = tv_ref[i] > 0

    @pl.when(jnp.logical_and(valid, j == 0))
    def _():
        acc_ref[...] = jnp.zeros_like(acc_ref)

    @pl.when(valid)
    def _():
        x = x_ref[...]
        g = jnp.dot(x, wg_ref[...].astype(BF16), preferred_element_type=F32)
        u = jnp.dot(x, wu_ref[...].astype(BF16), preferred_element_type=F32)
        a = (g * jax.nn.sigmoid(g)) * u
        acc_ref[...] += jnp.dot(a.astype(BF16), wd_ref[...].astype(BF16), preferred_element_type=F32)

    @pl.when(jnp.logical_and(valid, j == last))
    def _():
        out = acc_ref[...]
        if has_res:
            out = out + res_ref[...]
        o_ref[...] = out

    @pl.when(jnp.logical_and(jnp.logical_not(valid), j == last))
    def _():
        o_ref[...] = jnp.zeros_like(o_ref)


def _ffn(x, wg, wu, wd, tile_expert, tile_valid, tm, tf, name, res=None):
    n, d = x.shape
    f = wg.shape[2]
    assert x.dtype == BF16 and f % tf == 0
    nj = f // tf

    def fj(i, j, tv):
        return jnp.where(tv[i] > 0, j, nj - 1)

    in_specs = [pl.BlockSpec((tm, d), lambda i, j, te, tv: (i, 0)),
                pl.BlockSpec((None, d, tf), lambda i, j, te, tv: (te[i], 0, fj(i, j, tv))),
                pl.BlockSpec((None, d, tf), lambda i, j, te, tv: (te[i], 0, fj(i, j, tv))),
                pl.BlockSpec((None, tf, d), lambda i, j, te, tv: (te[i], fj(i, j, tv), 0))]
    args = [x, wg, wu, wd]
    if res is not None:
        in_specs.append(pl.BlockSpec((tm, d), lambda i, j, te, tv: (i, 0)))
        args.append(res)
    return pl.pallas_call(
        functools.partial(_ffn_kernel, has_res=res is not None),
        grid_spec=pltpu.PrefetchScalarGridSpec(
            num_scalar_prefetch=2,
            grid=(n // tm, nj),
            in_specs=in_specs,
            out_specs=pl.BlockSpec((tm, d), lambda i, j, te, tv: (i, 0)),
            scratch_shapes=[pltpu.VMEM((tm, d), F32)]),
        out_shape=jax.ShapeDtypeStruct((n, d), F32),
        compiler_params=_params("parallel", "arbitrary"),
        name=name,
    )(tile_expert, tile_valid, *args)


def _router_kernel(h_ref, w_ref, b_ref, idx_ref, gate_ref, cnt_ref, run_ref, *, tm):
    i = pl.program_id(0)

    @pl.when(i == 0)
    def _():
        run_ref[...] = jnp.zeros_like(run_ref)

    logits = jnp.dot(h_ref[...], w_ref[...], preferred_element_type=F32) + b_ref[...]
    lane = lax.broadcasted_iota(jnp.int32, logits.shape, 1)
    m1 = jnp.max(logits, axis=1, keepdims=True)
    i1 = jnp.min(jnp.where(logits == m1, lane, LANES), axis=1, keepdims=True)
    oh1 = lane == i1
    rest = jnp.where(oh1, -jnp.inf, logits)
    m2 = jnp.max(rest, axis=1, keepdims=True)
    i2 = jnp.min(jnp.where(rest == m2, lane, LANES), axis=1, keepdims=True)
    oh2 = lane == i2
    e = jnp.exp(m2 - m1)
    g1 = 1.0 / (1.0 + e)
    g2 = e / (1.0 + e)

    row = lax.broadcasted_iota(jnp.int32, (tm, tm), 0)
    col = lax.broadcasted_iota(jnp.int32, (tm, tm), 1)
    tri = (col < row).astype(BF16)
    picked = jnp.logical_or(oh1, oh2).astype(BF16)
    before = jnp.dot(tri, picked, preferred_element_type=F32) + run_ref[...]
    r1 = jnp.sum(jnp.where(oh1, before, 0.0), axis=1, keepdims=True).astype(jnp.int32)
    r2 = jnp.sum(jnp.where(oh2, before, 0.0), axis=1, keepdims=True).astype(jnp.int32)
    run_ref[...] += jnp.sum(picked.astype(F32), axis=0, keepdims=True)

    idx_ref[...] = jnp.where(lane == 0, i1, jnp.where(lane == 1, i2,
                             jnp.where(lane == 2, r1, jnp.where(lane == 3, r2, 0))))
    gate_ref[...] = jnp.where(lane == 0, g1, jnp.where(lane == 1, g2, 0.0))
    cnt_ref[...] = run_ref[...]


def _router(h, w_r, b_r):
    n, d = h.shape
    ne = w_r.shape[1]
    tm = _tile(n, TOKEN_TILE)
    w_pad = jnp.zeros((d, LANES), BF16).at[:, :ne].set(w_r.astype(BF16))
    b_pad = jnp.full((1, LANES), NEG_INF, F32).at[0, :ne].set(b_r.astype(F32))
    return pl.pallas_call(
        functools.partial(_router_kernel, tm=tm),
        grid=(n // tm,),
        in_specs=[pl.BlockSpec((tm, d), lambda i: (i, 0)),
                  pl.BlockSpec((d, LANES), lambda i: (0, 0)),
                  pl.BlockSpec((1, LANES), lambda i: (0, 0))],
        out_specs=[pl.BlockSpec((tm, LANES), lambda i: (i, 0)),
                   pl.BlockSpec((tm, LANES), lambda i: (i, 0)),
                   pl.BlockSpec((1, LANES), lambda i: (0, 0))],
        out_shape=[jax.ShapeDtypeStruct((n, LANES), jnp.int32),
                   jax.ShapeDtypeStruct((n, LANES), F32),
                   jax.ShapeDtypeStruct((1, LANES), F32)],
        scratch_shapes=[pltpu.VMEM((1, LANES), F32)],
        compiler_params=_params("arbitrary"),
        name="moe_router",
    )(h, w_pad, b_pad)


def _row_copy(src, src_row, dst, dst_row, sem):
    return pltpu.make_async_copy(src.at[pl.ds(src_row, 1), :], dst.at[pl.ds(dst_row, 1), :], sem)


def _dispatch_kernel(src_ref, h_ref, o_ref, buf_ref, sem, *, tt):
    def issue(r, c):
        _row_copy(h_ref, src_ref[0, 0, r], buf_ref, r, sem).start()
        return c

    lax.fori_loop(0, tt, issue, 0, unroll=8)

    def drain(r, c):
        _row_copy(h_ref, 0, buf_ref, 0, sem).wait()
        return c

    lax.fori_loop(0, tt, drain, 0, unroll=8)
    o_ref[...] = buf_ref[...].astype(o_ref.dtype)


def _dispatch(h, src):
    d = h.shape[1]
    n_slots = src.shape[0]
    tt = _tile(n_slots, ROW_DMA_BYTES // (d * h.dtype.itemsize))
    return pl.pallas_call(
        functools.partial(_dispatch_kernel, tt=tt),
        grid=(n_slots // tt,),
        in_specs=[pl.BlockSpec((1, 1, tt), lambda i: (i, 0, 0), memory_space=pltpu.SMEM),
                  pl.BlockSpec(memory_space=pl.ANY)],
        out_specs=pl.BlockSpec((tt, d), lambda i: (i, 0)),
        out_shape=jax.ShapeDtypeStruct((n_slots, d), BF16),
        scratch_shapes=[pltpu.VMEM((tt, d), h.dtype), pltpu.SemaphoreType.DMA(())],
        compiler_params=_params("arbitrary"),
        name="moe_dispatch",
    )(src.reshape(n_slots // tt, 1, tt), h)


def _combine_kernel(pos_ref, x_ref, gate_ref, y_ref, g_ref, *rest, tt, want_x):
    if want_x:
        x_out_ref, o_ref, buf_ref, sem = rest
    else:
        o_ref, buf_ref, sem = rest

    def issue(r, c):
        for k in range(TOP_K):
            _row_copy(y_ref, pos_ref[0, 0, TOP_K * r + k], buf_ref.at[k], r, sem).start()
        return c

    lax.fori_loop(0, tt, issue, 0, unroll=8)

    def drain(r, c):
        for k in range(TOP_K):
            _row_copy(y_ref, 0, buf_ref.at[k], 0, sem).wait()
        return c

    lax.fori_loop(0, tt, drain, 0, unroll=8)
    g = gate_ref[...]
    x = x_ref[...] + (g[:, 0:1] * buf_ref[0] + g[:, 1:2] * buf_ref[1])
    if want_x:
        x_out_ref[...] = x
    o_ref[...] = _rms(x, g_ref[...]).astype(o_ref.dtype)


def _combine_norm(x, gates, y, pos, g, row0, n_rows, norm_dtype, want_x):
    n, d = x.shape
    tt = _tile(n_rows, ROW_DMA_BYTES // (TOP_K * d * y.dtype.itemsize))
    assert n % tt == 0 and row0 % tt == 0
    off = row0 // tt
    pos3 = pos.reshape(n // tt, 1, TOP_K * tt)
    out = pl.BlockSpec((tt, d), lambda i: (i, 0))
    return pl.pallas_call(
        functools.partial(_combine_kernel, tt=tt, want_x=want_x),
        grid=(n_rows // tt,),
        in_specs=[pl.BlockSpec((1, 1, TOP_K * tt), lambda i: (off + i, 0, 0), memory_space=pltpu.SMEM),
                  pl.BlockSpec((tt, d), lambda i: (off + i, 0)),
                  pl.BlockSpec((tt, LANES), lambda i: (off + i, 0)),
                  pl.BlockSpec(memory_space=pl.ANY),
                  pl.BlockSpec((1, d), lambda i: (0, 0))],
        out_specs=[out] * (2 if want_x else 1),
        out_shape=([jax.ShapeDtypeStruct((n_rows, d), F32)] if want_x else [])
                  + [jax.ShapeDtypeStruct((n_rows, d), norm_dtype)],
        scratch_shapes=[pltpu.VMEM((TOP_K, tt, d), F32), pltpu.SemaphoreType.DMA(())],
        compiler_params=_params("arbitrary"),
        name="moe_combine",
    )(pos3, x, gates, y, g.reshape(1, d).astype(F32))


def _moe_route(h_bf, w_r, b_r, ne, tm):
    n = h_bf.shape[0]
    idx, gates, counts = _router(h_bf, w_r, b_r)
    counts = counts[0, :ne].astype(jnp.int32)
    padded = ((counts + tm - 1) // tm) * tm
    ends = jnp.cumsum(padded)
    offsets = ends - padded
    expert = idx[:, :TOP_K]
    pos = jnp.zeros_like(expert)
    for e in range(ne):
        pos = jnp.where(expert == e, offsets[e], pos)
    pos = pos + idx[:, TOP_K:2 * TOP_K]
    n_tiles = -(-(n * TOP_K) // tm) + ne
    tile_start = jnp.arange(n_tiles, dtype=jnp.int32) * tm
    tile_valid = (tile_start < ends[-1]).astype(jnp.int32)
    tile_expert = jnp.minimum(jnp.sum((tile_start[:, None] >= ends[None, :]).astype(jnp.int32), axis=1),
                              ne - 1)
    last_expert = jnp.max(jnp.where(counts > 0, jnp.arange(ne, dtype=jnp.int32), 0))
    tile_expert = jnp.where(tile_valid > 0, tile_expert, last_expert)
    token = jnp.broadcast_to(jnp.arange(n, dtype=jnp.int32)[:, None], pos.shape)
    src = jnp.zeros((n_tiles * tm,), jnp.int32).at[pos.reshape(-1)].set(
        token.reshape(-1), unique_indices=True)
    return pos, src, gates, tile_expert, tile_valid


def kernel(x_prompt, x_sample, cache_diff_k, cache_diff_v, cache_sb_k, cache_sb_v, rel_bias, norm_mix_g, w_in, lambda_q1, lambda_k1, lambda_q2, lambda_k2, subln_g, w_branch_a, w_branch_b, w_out, norm_ffn_g, dense_w_gate, dense_w_up, dense_w_down, router_w, router_b, moe_w_gate, moe_w_up, moe_w_down, final_norm_g):
    batch, seq, d = x_prompt.shape
    db, ts, _ = x_sample.shape
    depth, _, past, heads, w2 = cache_diff_k.shape
    d_a = w2 // 2
    d_b = cache_sb_k.shape[-1]
    wid = heads * w2
    n_p = batch * seq
    n_s = db * ts
    n_tok = n_p + n_s
    assert d % wid == 0 and w_in.shape[2] == 6 * wid + 2 * d
    assert cache_sb_k.shape[3] * d_b == wid and cache_diff_v.shape[-1] == w2

    x, h = _join_norm(x_prompt.reshape(n_p, d), x_sample.reshape(n_s, d), norm_mix_g[0])

    t_attn = _tile(seq, ATTN_TILE)
    pos_q = past + jnp.arange(ts, dtype=jnp.int32)
    bias_tiles = _prompt_bias_tiles(rel_bias, t_attn)
    bias_c = _masked_bias(rel_bias, pos_q, jnp.arange(past, dtype=jnp.int32))
    bias_n = _masked_bias(rel_bias, pos_q, pos_q)

    caches = [c.reshape(depth, db, past * heads, w2)
              for c in (cache_diff_k, cache_diff_v, cache_sb_k, cache_sb_v)]
    cache_p = [None] * 4
    cache_s = [None] * 4
    kv_cols = (1, 2, 4, 5)
    for l in range(depth):
        lam_init = 0.8 - 0.6 * math.exp(-0.3 * l)
        (q,) = _proj(h, w_in, l, lambda j: 3 * j, 2, wid,
                     [((n_tok, 2 * wid), BF16, (_tile(n_tok, TOKEN_TILE), wid), lambda j, i: (i, j))],
                     "proj_q")
        (gates,) = _proj(h, w_in, l, lambda j: j + 6, 2 * d // wid, wid,
                         [((n_tok, 2 * d), F32, (_tile(n_tok, TOKEN_TILE), wid), lambda j, i: (i, j))],
                         "proj_gate")
        for c in range(4):
            cache_p[c] = _proj_cache_rows(h, w_in, l, kv_cols[c], wid, 0, n_p, cache_p[c], "proj_kv_prompt")
            cache_s[c] = _proj_cache_rows(h, w_in, l, kv_cols[c], wid, n_p, n_s, cache_s[c], "proj_kv_sample")

        lam_vecs = [v[l].reshape(1, d_a).astype(F32) for v in (lambda_q1, lambda_k1, lambda_q2, lambda_k2)]
        g_sub = subln_g[l].reshape(1, w2).astype(F32)
        oa, ob = _prompt_attention(q, cache_p, l, bias_tiles, lam_vecs, g_sub, n_tok, batch, seq, heads, d_a,
                                   d_b, t_attn, lam_init)
        oa, ob = _sample_attention(q, cache_s, caches, l, bias_c, bias_n, lam_vecs, g_sub, oa, ob, n_p, db,
                                   ts, heads, d_a, d_b, lam_init)
        merged = _merge(oa, ob, gates, w_branch_a[l].astype(BF16), w_branch_b[l].astype(BF16))

        j = l // 2
        last = l + 1 == depth
        tm = _tile(n_tok, TOKEN_TILE)
        if l % 2 == 0:
            x, h2 = _out_proj(merged, w_out[l].astype(BF16), x, norm_ffn_g[l], [BF16])
            n_tiles = n_tok // tm
            x = _ffn(h2, dense_w_gate[j:j + 1].astype(BF16), dense_w_up[j:j + 1].astype(BF16),
                     dense_w_down[j:j + 1].astype(BF16), jnp.zeros((n_tiles,), jnp.int32),
                     jnp.ones((n_tiles,), jnp.int32), tm, _tile(dense_w_gate.shape[2], FF_TILE),
                     "dense_ffn", res=x)
            if last:
                (y_prompt,) = _rmsnorm(x, final_norm_g, [F32], "norm_final", 0, n_p)
                (y_sample,) = _rmsnorm(x, final_norm_g, [F32], "norm_final", n_p, n_s)
            else:
                (h,) = _rmsnorm(x, norm_mix_g[l + 1], [BF16], "norm_mix")
        else:
            x, h2, h2_f32 = _out_proj(merged, w_out[l].astype(BF16), x, norm_ffn_g[l], [BF16, F32])
            n_moe, ne, _, f_e = moe_w_gate.shape
            tm_e = MOE_ROW_TILE
            pos, src, route_g, tile_expert, tile_valid = _moe_route(h2, router_w[j], router_b[j], ne, tm_e)
            xs = _dispatch(h2_f32, src)
            y = _ffn(xs, moe_w_gate.reshape(n_moe * ne, d, f_e), moe_w_up.reshape(n_moe * ne, d, f_e),
                     moe_w_down.reshape(n_moe * ne, f_e, d), tile_expert + j * ne, tile_valid, tm_e,
                     _tile(f_e, MOE_FF_TILE), "moe_ffn")
            if last:
                (y_prompt,) = _combine_norm(x, route_g, y, pos, final_norm_g, 0, n_p, F32, False)
                (y_sample,) = _combine_norm(x, route_g, y, pos, final_norm_g, n_p, n_s, F32, False)
            else:
                x, h = _combine_norm(x, route_g, y, pos, norm_mix_g[l + 1], 0, n_tok, BF16, True)

    outs_p = [c.reshape(depth, batch, seq, heads, w2) for c in cache_p]
    outs_s = [c.reshape(depth, db, ts, heads, w2) for c in cache_s]
    return (y_prompt.reshape(batch, seq, d), y_sample.reshape(db, ts, d), *outs_p, *outs_s)
```

```python
import functools
import math

import jax
import jax.numpy as jnp
from jax import lax
from jax.experimental import pallas as pl
from jax.experimental.pallas import tpu as pltpu

CHUNK = 64
NUM_BUCKETS = 32
MAX_DISTANCE = 128
TOP_K = 2
EPS = 1e-6
NEG_INF = -1e30

LANES = 128
VMEM_LIMIT_BYTES = 56 * 1024 * 1024
TOKEN_TILE = 512
ATTN_TILE = 512
FF_TILE = 512
MOE_ROW_TILE = 1024
MOE_FF_TILE = 256
ROW_DMA_BYTES = 4 * 1024 * 1024

BF16 = jnp.bfloat16
F32 = jnp.float32


def _params(*sem):
    return pltpu.CompilerParams(dimension_semantics=sem, vmem_limit_bytes=VMEM_LIMIT_BYTES)


def _tile(n, want):
    t = min(n, want)
    while n % t:
        t //= 2
    return t


def _rms(x, g):
    return x * lax.rsqrt(jnp.mean(x * x, axis=-1, keepdims=True) + EPS) * g


def _rmsnorm_kernel(x_ref, g_ref, *o_refs):
    y = _rms(x_ref[...].astype(F32), g_ref[...])
    for o_ref in o_refs:
        o_ref[...] = y.astype(o_ref.dtype)


def _rmsnorm(x, g, out_dtypes, name, row0=0, n_rows=None):
    d = x.shape[1]
    n_rows = x.shape[0] if n_rows is None else n_rows
    tm = _tile(n_rows, TOKEN_TILE)
    assert row0 % tm == 0
    off = row0 // tm
    return pl.pallas_call(
        _rmsnorm_kernel,
        grid=(n_rows // tm,),
        in_specs=[pl.BlockSpec((tm, d), lambda i: (off + i, 0)),
                  pl.BlockSpec((1, d), lambda i: (0, 0))],
        out_specs=[pl.BlockSpec((tm, d), lambda i: (i, 0)) for _ in out_dtypes],
        out_shape=[jax.ShapeDtypeStruct((n_rows, d), dt) for dt in out_dtypes],
        compiler_params=_params("parallel"),
        name=name,
    )(x, g.reshape(1, d).astype(F32))


def _join_norm_kernel(xp_ref, xs_ref, g_ref, x_ref, h_ref, *, np_tiles):
    x = jnp.where(pl.program_id(0) < np_tiles, xp_ref[...], xs_ref[...])
    x_ref[...] = x
    h_ref[...] = _rms(x, g_ref[...]).astype(h_ref.dtype)


def _join_norm(xp, xs, g):
    (n_p, d), n_s = xp.shape, xs.shape[0]
    tm = _tile(n_s, TOKEN_TILE)
    assert n_p % tm == 0
    np_tiles = n_p // tm
    out = pl.BlockSpec((tm, d), lambda i: (i, 0))
    return pl.pallas_call(
        functools.partial(_join_norm_kernel, np_tiles=np_tiles),
        grid=((n_p + n_s) // tm,),
        in_specs=[pl.BlockSpec((tm, d), lambda i: (jnp.minimum(i, np_tiles - 1), 0)),
                  pl.BlockSpec((tm, d), lambda i: (jnp.maximum(i - np_tiles, 0), 0)),
                  pl.BlockSpec((1, d), lambda i: (0, 0))],
        out_specs=[out, out],
        out_shape=[jax.ShapeDtypeStruct((n_p + n_s, d), F32), jax.ShapeDtypeStruct((n_p + n_s, d), BF16)],
        compiler_params=_params("parallel"),
        name="join_norm",
    )(xp, xs, g.reshape(1, d).astype(F32))


def _proj_kernel(x_ref, w_ref, *rest, n_out):
    o_refs, wb_ref = rest[-1 - n_out:-1], rest[-1]

    @pl.when(pl.program_id(1) == 0)
    def _():
        wb_ref[...] = w_ref[...].astype(BF16)

    acc = jnp.dot(x_ref[...], wb_ref[...], preferred_element_type=F32)
    for o_ref in o_refs:
        o_ref[...] = acc.astype(o_ref.dtype)


def _proj(x, w, layer, col_block_fn, n_blocks, tn, outs, name, row0=0, n_rows=None):
    k = x.shape[1]
    n_rows = x.shape[0] if n_rows is None else n_rows
    tm = _tile(n_rows, TOKEN_TILE)
    assert row0 % tm == 0
    off = row0 // tm
    return pl.pallas_call(
        functools.partial(_proj_kernel, n_out=len(outs)),
        grid=(n_blocks, n_rows // tm),
        in_specs=[pl.BlockSpec((tm, k), lambda j, i: (off + i, 0)),
                  pl.BlockSpec((None, k, tn), lambda j, i: (layer, 0, col_block_fn(j)))],
        out_specs=[pl.BlockSpec(o[2], o[3]) for o in outs],
        out_shape=[jax.ShapeDtypeStruct(o[0], o[1]) for o in outs],
        scratch_shapes=[pltpu.VMEM((k, tn), BF16)],
        compiler_params=_params("parallel", "arbitrary"),
        name=name,
    )(x, w)


def _cache_rows_kernel(x_ref, w_ref, *rest):
    o_ref, wb_ref = rest[-2:]
    s = pl.program_id(0)

    @pl.when(jnp.logical_and(s == 0, pl.program_id(1) == 0))
    def _():
        wb_ref[...] = w_ref[...].astype(BF16)

    @pl.when(s == 0)
    def _():
        o_ref[...] = jnp.dot(x_ref[...], wb_ref[...], preferred_element_type=F32)

    @pl.when(s > 0)
    def _():
        o_ref[...] = jnp.zeros_like(o_ref)


def _proj_cache_rows(h, w_in, layer, col, wid, row0, n_rows, prev, name):
    depth, k, _ = w_in.shape
    tm = _tile(n_rows, TOKEN_TILE)
    assert row0 % tm == 0
    off = row0 // tm
    nt = n_rows // tm
    tile = lambda s, i: jnp.where(s == 0, i, nt - 1)
    in_specs = [pl.BlockSpec((tm, k), lambda s, i: (off + tile(s, i), 0)),
                pl.BlockSpec((None, k, wid), lambda s, i: (layer, 0, col))]
    args = [h, w_in]
    if prev is not None:
        in_specs.append(pl.BlockSpec(memory_space=pl.ANY))
        args.append(prev)
    return pl.pallas_call(
        _cache_rows_kernel,
        grid=(1 if prev is not None else depth, nt),
        in_specs=in_specs,
        out_specs=pl.BlockSpec((None, tm, wid), lambda s, i: ((layer + s) % depth, i, 0)),
        out_shape=jax.ShapeDtypeStruct((depth, n_rows, wid), F32),
        scratch_shapes=[pltpu.VMEM((k, wid), BF16)],
        input_output_aliases={2: 0} if prev is not None else {},
        compiler_params=_params("arbitrary", "arbitrary"),
        name=name,
    )(*args)


def _t5_bucket(rel):
    half = NUM_BUCKETS // 2
    max_exact = half // 2
    n = jnp.abs(rel)
    far = max_exact + (jnp.log(jnp.maximum(n, 1).astype(F32) / max_exact)
                       / math.log(MAX_DISTANCE / max_exact) * (half - max_exact)).astype(jnp.int32)
    far = jnp.minimum(far, half - 1)
    return jnp.where(rel > 0, half, 0) + jnp.where(n < max_exact, n, far)


def _masked_bias(rel_bias, q_pos, k_pos):
    bucket = _t5_bucket(k_pos[None, :] - q_pos[:, None])
    bias = jnp.zeros((rel_bias.shape[1],) + bucket.shape, F32)
    for b in range(NUM_BUCKETS):
        bias = jnp.where(bucket[None] == b, rel_bias[b].astype(F32)[:, None, None], bias)
    mask = (k_pos[None, :] // CHUNK) <= (q_pos[:, None] // CHUNK)
    return jnp.where(mask[None], bias, NEG_INF)


def _prompt_bias_tiles(rel_bias, t):
    assert t >= MAX_DISTANCE and t % CHUNK == 0
    pos = jnp.arange(t, dtype=jnp.int32)
    tiles = [_masked_bias(rel_bias, 2 * t + pos, (2 - d) * t + pos) for d in (2, 1, 0)]
    return jnp.stack(tiles, axis=1)


def _split_maps(q, d_a):
    lane = lax.broadcasted_iota(jnp.int32, q.shape, 1)
    zero = jnp.zeros_like(q)
    return jnp.concatenate([jnp.where(lane < d_a, q, zero), jnp.where(lane >= d_a, q, zero)], axis=0)


def _lambda(lq1, lk1, lq2, lk2, lam_init):
    return (jnp.exp(jnp.sum(lq1 * lk1, axis=-1, keepdims=True))
            - jnp.exp(jnp.sum(lq2 * lk2, axis=-1, keepdims=True)) + lam_init)


def _diff_finish(o, lam, g, t, lam_init):
    oa = o[:t] - lam * o[t:]
    return _rms(oa, g) * (1.0 - lam_init)


def _nt_dot(a, b):
    return lax.dot_general(a, b, (((1,), (1,)), ((), ())), preferred_element_type=F32)


def _diff_prompt_kernel(q_ref, k_ref, v_ref, bias_ref, lq1_ref, lk1_ref, lq2_ref, lk2_ref, g_ref,
                        o_in_ref, o_ref, *, t, d_a, lam_init):
    del o_in_ref
    qi = pl.program_id(2)
    scale = d_a ** -0.5
    q = q_ref[...]
    exact_scale = math.frexp(scale)[0] == 0.5
    if exact_scale:
        q = (q.astype(F32) * scale).astype(q.dtype)
    qq = _split_maps(q, d_a)
    dv = v_ref.shape[-1]
    half = t // 2
    assert half % CHUNK == 0

    def update(sc, bias, state, vb):
        m, l, acc = state
        sc = (sc if exact_scale else sc * scale) + bias
        m_new = jnp.maximum(m, jnp.max(sc, axis=1, keepdims=True))
        alpha = jnp.exp(m - m_new)
        p = jnp.exp(sc - m_new)
        l = alpha * l + jnp.sum(p, axis=1, keepdims=True)
        acc = alpha * acc + jnp.dot(p.astype(BF16), vb, preferred_element_type=F32)
        return m_new, l, acc

    def body(j, carry):
        start = pl.multiple_of(j * t, t)
        vb = v_ref[pl.ds(start, t), :].astype(BF16)
        bt = bias_ref[jnp.clip(j - qi + 2, 0, 1)]
        s = _nt_dot(qq, k_ref[pl.ds(start, t), :].astype(BF16))
        return tuple(update(s[c * t:(c + 1) * t], bt, carry[c], vb) for c in range(2))

    one = (jnp.full((t, 1), NEG_INF, F32), jnp.zeros((t, 1), F32), jnp.zeros((t, dv), F32))
    carry = lax.fori_loop(0, qi, body, (one, one))

    start = pl.multiple_of(qi * t, t)
    bt = bias_ref[2]
    rows = lambda x, c, r: x[c * t + r * half:c * t + (r + 1) * half]
    q_top = jnp.concatenate([rows(qq, 0, 0), rows(qq, 1, 0)], axis=0)
    q_bot = jnp.concatenate([rows(qq, 0, 1), rows(qq, 1, 1)], axis=0)
    s_top = _nt_dot(q_top, k_ref[pl.ds(start, half), :].astype(BF16))
    s_bot = _nt_dot(q_bot, k_ref[pl.ds(start, t), :].astype(BF16))
    outs = []
    for c in range(2):
        state = [tuple(x[r * half:(r + 1) * half] for x in carry[c]) for r in range(2)]
        top = update(s_top[c * half:(c + 1) * half], bt[:half, :half], state[0],
                     v_ref[pl.ds(start, half), :].astype(BF16))
        bot = update(s_bot[c * half:(c + 1) * half], bt[half:, :], state[1], v_ref[pl.ds(start, t), :].astype(BF16))
        outs.append(jnp.concatenate([top[2] / top[1], bot[2] / bot[1]], axis=0))
    lam = _lambda(lq1_ref[...], lk1_ref[...], lq2_ref[...], lk2_ref[...], lam_init)
    o = jnp.concatenate(outs, axis=0)
    o_ref[...] = _diff_finish(o, lam, g_ref[...], t, lam_init).astype(o_ref.dtype)


def _head_rows(ref, h, heads, start, size):
    return ref[pl.ds(start * heads + h, size, stride=heads), :]


def _diff_sample_kernel(q_ref, kc_ref, vc_ref, kn_ref, vn_ref, bc_ref, bn_ref, lq1_ref, lk1_ref,
                        lq2_ref, lk2_ref, g_ref, o_in_ref, o_ref, *, t, d_a, lam_init, heads):
    del o_in_ref
    scale = d_a ** -0.5
    w2 = 2 * d_a
    past = kc_ref.shape[0] // heads
    lam = _lambda(lq1_ref[...], lk1_ref[...], lq2_ref[...], lk2_ref[...], lam_init)
    for h in range(heads):
        cols = slice(h * w2, (h + 1) * w2)
        qq = _split_maps(q_ref[:, cols], d_a)
        bc = bc_ref[h]
        bn = bn_ref[h]
        kc = _head_rows(kc_ref, h, heads, 0, past).astype(BF16)
        vc = _head_rows(vc_ref, h, heads, 0, past).astype(BF16)
        s_c = _nt_dot(qq, kc) * scale + jnp.concatenate([bc, bc], axis=0)
        s_n = _nt_dot(qq, kn_ref[:, cols].astype(BF16)) * scale + jnp.concatenate([bn, bn], axis=0)
        m = jnp.maximum(jnp.max(s_c, axis=1, keepdims=True), jnp.max(s_n, axis=1, keepdims=True))
        p_c = jnp.exp(s_c - m)
        p_n = jnp.exp(s_n - m)
        l = jnp.sum(p_c, axis=1, keepdims=True) + jnp.sum(p_n, axis=1, keepdims=True)
        acc = (jnp.dot(p_c.astype(BF16), vc, preferred_element_type=F32)
               + jnp.dot(p_n.astype(BF16), vn_ref[:, cols].astype(BF16), preferred_element_type=F32))
        o_ref[:, cols] = _diff_finish(acc / l, lam, g_ref[...], t, lam_init).astype(o_ref.dtype)


def _triangle(n):
    row = lax.broadcasted_iota(jnp.int32, (n, n), 0)
    col = lax.broadcasted_iota(jnp.int32, (n, n), 1)
    return (row > col).astype(BF16)


def _suffix_sums(x, tri):
    hi = x.astype(BF16)
    lo = (x - hi.astype(F32)).astype(BF16)
    return jnp.dot(hi, tri, preferred_element_type=F32) + jnp.dot(lo, tri, preferred_element_type=F32)


def _sb_span(q, kb, vb, r_sum, acc, scale, causal, n_parts):
    tk = kb.shape[0]
    part = tk // n_parts
    z = _nt_dot(q, kb) * scale
    sp = jnp.log(1.0 + jnp.exp(-jnp.abs(z)))
    ls = jnp.minimum(z, 0.0) - sp
    l1m = ls - z
    if causal is not None:
        l1m = jnp.where(causal, l1m, 0.0)
    tri = _triangle(part)
    between = [None] * n_parts
    for n in reversed(range(n_parts)):
        x = l1m[:, n * part:(n + 1) * part]
        between[n] = _suffix_sums(x, tri) + r_sum
        r_sum = r_sum + jnp.sum(x, axis=1, keepdims=True)
    a = jnp.exp(ls + jnp.concatenate(between, axis=1))
    if causal is not None:
        a = jnp.where(causal, a, 0.0)
    acc = acc + jnp.dot(a.astype(BF16), vb, preferred_element_type=F32)
    return r_sum, acc


def _strict_causal(tq, tk):
    row = lax.broadcasted_iota(jnp.int32, (tq, tk), 0)
    col = lax.broadcasted_iota(jnp.int32, (tq, tk), 1)
    return col < row


def _sb_parts(t):
    return max(1, t // 256)


def _sb_prompt_kernel(q_ref, k_ref, v_ref, o_in_ref, o_ref, *, t, d_b):
    del o_in_ref
    qi = pl.program_id(2)
    q = q_ref[...]
    scale = d_b ** -0.5
    dv = v_ref.shape[-1]
    parts = _sb_parts(t)
    start = pl.multiple_of(qi * t, t)
    half = t // 2
    zeros = (jnp.zeros((half, 1), F32), jnp.zeros((half, dv), F32))
    top = _sb_span(q[:half], k_ref[pl.ds(start, half), :].astype(BF16), v_ref[pl.ds(start, half), :].astype(BF16), *zeros,
                   scale, _strict_causal(half, half), _sb_parts(half))
    row = lax.broadcasted_iota(jnp.int32, (half, t), 0) + half
    col = lax.broadcasted_iota(jnp.int32, (half, t), 1)
    bottom = _sb_span(q[half:], k_ref[pl.ds(start, t), :].astype(BF16), v_ref[pl.ds(start, t), :].astype(BF16), *zeros,
                      scale, col < row, parts)
    carry = tuple(jnp.concatenate([a, b], axis=0) for a, b in zip(top, bottom))

    def body(n, carry):
        s = pl.multiple_of((qi - 1 - n) * t, t)
        return _sb_span(q, k_ref[pl.ds(s, t), :].astype(BF16), v_ref[pl.ds(s, t), :].astype(BF16),
                        carry[0], carry[1],
                        scale, None, parts)

    _, acc = lax.fori_loop(0, qi, body, carry)
    o_ref[...] = acc.astype(o_ref.dtype)


def _sb_sample_kernel(q_ref, kc_ref, vc_ref, kn_ref, vn_ref, o_in_ref, o_ref, *, t, d_b, tc, heads):
    del o_in_ref
    scale = d_b ** -0.5
    past = kc_ref.shape[0] // heads
    cols = [slice(h * d_b, (h + 1) * d_b) for h in range(heads)]
    qs = [q_ref[:, c] for c in cols]
    causal = _strict_causal(t, t)
    carry = tuple(_sb_span(qs[h], kn_ref[:, c].astype(BF16), vn_ref[:, c].astype(BF16),
                           jnp.zeros((t, 1), F32), jnp.zeros((t, d_b), F32), scale, causal, 1)
                  for h, c in enumerate(cols))

    def body(n, carry):
        s = pl.multiple_of(past - (n + 1) * tc, tc)
        return tuple(_sb_span(qs[h], _head_rows(kc_ref, h, heads, s, tc).astype(BF16),
                              _head_rows(vc_ref, h, heads, s, tc).astype(BF16), carry[h][0], carry[h][1],
                              scale, None, _sb_parts(tc))
                     for h in range(heads))

    carry = lax.fori_loop(0, past // tc, body, carry)
    for h, c in enumerate(cols):
        o_ref[:, c] = carry[h][1].astype(o_ref.dtype)


def _prompt_attention(q, kv, layer, bias_tiles, lam_vecs, subln_g, n_tok, batch, seq, heads, d_a, d_b, t,
                      lam_init):
    nq = seq // t
    grid = (batch, heads, nq)
    w2 = 2 * d_a
    q_spec = lambda off: pl.BlockSpec((t, w2), lambda b, h, i: (b * nq + i, off + h))
    kv_spec = pl.BlockSpec((None, seq, w2), lambda b, h, i: (layer, b, h))
    o_spec = pl.BlockSpec((t, w2), lambda b, h, i: (b * nq + i, h))
    o_shape = jax.ShapeDtypeStruct((n_tok, heads * w2), BF16)
    vec = pl.BlockSpec((1, d_a), lambda b, h, i: (0, 0))
    any_spec = pl.BlockSpec(memory_space=pl.ANY)
    blank = jnp.zeros(o_shape.shape, o_shape.dtype)
    oa = pl.pallas_call(
        functools.partial(_diff_prompt_kernel, t=t, d_a=d_a, lam_init=lam_init),
        grid=grid,
        in_specs=[q_spec(0), kv_spec, kv_spec,
                  pl.BlockSpec((None, 3, t, t), lambda b, h, i: (h, 0, 0, 0)),
                  vec, vec, vec, vec,
                  pl.BlockSpec((1, w2), lambda b, h, i: (0, 0)),
                  any_spec],
        out_specs=o_spec, out_shape=o_shape,
        input_output_aliases={9: 0},
        compiler_params=_params("parallel", "parallel", "parallel"),
        name="diff_attn_prompt",
    )(q, kv[0], kv[1], bias_tiles, *lam_vecs, subln_g, blank)
    assert d_b == w2
    ob = pl.pallas_call(
        functools.partial(_sb_prompt_kernel, t=t, d_b=d_b),
        grid=grid,
        in_specs=[q_spec(heads), kv_spec, kv_spec, any_spec],
        out_specs=o_spec, out_shape=o_shape,
        input_output_aliases={3: 0},
        compiler_params=_params("parallel", "parallel", "parallel"),
        name="sb_attn_prompt",
    )(q, kv[2], kv[3], blank)
    return oa, ob


def _sample_attention(q, kv, caches, layer, bias_c, bias_n, lam_vecs, subln_g, oa, ob, n_p, db, ts,
                      heads, d_a, d_b, lam_init):
    w2 = 2 * d_a
    wid = heads * w2
    rows = caches[0].shape[2]
    past = rows // heads
    assert n_p % ts == 0
    row0 = n_p // ts
    grid = (db,)
    q_spec = lambda off: pl.BlockSpec((ts, wid), lambda b: (row0 + b, off))
    n_spec = pl.BlockSpec((None, ts, wid), lambda b: (layer, b, 0))
    c_spec = pl.BlockSpec((None, None, rows, w2), lambda b: (layer, b, 0, 0))
    o_spec = pl.BlockSpec((ts, wid), lambda b: (row0 + b, 0))
    any_spec = pl.BlockSpec(memory_space=pl.ANY)
    vec = pl.BlockSpec((1, d_a), lambda b: (0, 0))
    oa = pl.pallas_call(
        functools.partial(_diff_sample_kernel, t=ts, d_a=d_a, lam_init=lam_init, heads=heads),
        grid=grid,
        in_specs=[q_spec(0), c_spec, c_spec, n_spec, n_spec,
                  pl.BlockSpec((heads, ts, past), lambda b: (0, 0, 0)),
                  pl.BlockSpec((heads, ts, ts), lambda b: (0, 0, 0)),
                  vec, vec, vec, vec,
                  pl.BlockSpec((1, w2), lambda b: (0, 0)),
                  any_spec],
        out_specs=o_spec, out_shape=jax.ShapeDtypeStruct(oa.shape, oa.dtype),
        input_output_aliases={12: 0},
        compiler_params=_params("parallel"),
        name="diff_attn_sample",
    )(q, caches[0], caches[1], kv[0], kv[1], bias_c, bias_n, *lam_vecs, subln_g, oa)
    tc = _tile(past, 1024)
    ob = pl.pallas_call(
        functools.partial(_sb_sample_kernel, t=ts, d_b=d_b, tc=tc, heads=heads),
        grid=grid,
        in_specs=[q_spec(1), c_spec, c_spec, n_spec, n_spec, any_spec],
        out_specs=o_spec, out_shape=jax.ShapeDtypeStruct(ob.shape, ob.dtype),
        input_output_aliases={5: 0},
        compiler_params=_params("parallel"),
        name="sb_attn_sample",
    )(q, caches[2], caches[3], kv[2], kv[3], ob)
    return oa, ob


def _merge_kernel(oa_ref, ob_ref, g_ref, wa_ref, wb_ref, o_ref, *, d):
    a = jnp.dot(oa_ref[...], wa_ref[...], preferred_element_type=F32)
    b = jnp.dot(ob_ref[...], wb_ref[...], preferred_element_type=F32)
    ga = jax.nn.sigmoid(g_ref[:, :d])
    gb = jax.nn.sigmoid(g_ref[:, d:])
    o_ref[...] = (ga * a + gb * b).astype(o_ref.dtype)


def _merge(oa, ob, gates, wa, wb):
    n, w = oa.shape
    d = wa.shape[1]
    tm = _tile(n, TOKEN_TILE // 2)
    return pl.pallas_call(
        functools.partial(_merge_kernel, d=d),
        grid=(n // tm,),
        in_specs=[pl.BlockSpec((tm, w), lambda i: (i, 0)),
                  pl.BlockSpec((tm, w), lambda i: (i, 0)),
                  pl.BlockSpec((tm, 2 * d), lambda i: (i, 0)),
                  pl.BlockSpec((w, d), lambda i: (0, 0)),
                  pl.BlockSpec((w, d), lambda i: (0, 0))],
        out_specs=pl.BlockSpec((tm, d), lambda i: (i, 0)),
        out_shape=jax.ShapeDtypeStruct((n, d), BF16),
        compiler_params=_params("parallel"),
        name="branch_merge",
    )(oa, ob, gates, wa, wb)


def _out_proj_kernel(m_ref, w_ref, r_ref, g_ref, x_ref, *h_refs):
    x = jnp.dot(m_ref[...], w_ref[...], preferred_element_type=F32) + r_ref[...]
    x_ref[...] = x
    y = _rms(x, g_ref[...])
    for h_ref in h_refs:
        h_ref[...] = y.astype(h_ref.dtype)


def _out_proj(merged, w, res, g, norm_dtypes):
    n, d = res.shape
    k = merged.shape[1]
    tm = _tile(n, TOKEN_TILE // 2)
    row = lambda c: pl.BlockSpec((tm, c), lambda i: (i, 0))
    return pl.pallas_call(
        _out_proj_kernel,
        grid=(n // tm,),
        in_specs=[row(k), pl.BlockSpec((k, d), lambda i: (0, 0)), row(d),
                  pl.BlockSpec((1, d), lambda i: (0, 0))],
        out_specs=[row(d)] + [row(d) for _ in norm_dtypes],
        out_shape=[jax.ShapeDtypeStruct((n, d), F32)]
                  + [jax.ShapeDtypeStruct((n, d), dt) for dt in norm_dtypes],
        compiler_params=_params("parallel"),
        name="proj_out",
    )(merged, w, res, g.reshape(1, d).astype(F32))


def _ffn_kernel(te_ref, tv_ref, x_ref, wg_ref, wu_ref, wd_ref, *rest, has_res):
    if has_res:
        res_ref, o_ref, acc_ref = rest
    else:
        res_ref = None
        o_ref, acc_ref = rest
    i = pl.program_id(0)
    j = pl.program_id(1)
    last = pl.num_programs(1) - 1
    valid = tv_ref[i] > 0

    @pl.when(jnp.logical_and(valid, j == 0))
    def _():
        acc_ref[...] = jnp.zeros_like(acc_ref)

    @pl.when(valid)
    def _():
        x = x_ref[...]
        g = jnp.dot(x, wg_ref[...].astype(BF16), preferred_element_type=F32)
        u = jnp.dot(x, wu_ref[...].astype(BF16), preferred_element_type=F32)
        a = (g * jax.nn.sigmoid(g)) * u
        acc_ref[...] += jnp.dot(a.astype(BF16), wd_ref[...].astype(BF16), preferred_element_type=F32)

    @pl.when(jnp.logical_and(valid, j == last))
    def _():
        out = acc_ref[...]
        if has_res:
            out = out + res_ref[...]
        o_ref[...] = out

    @pl.when(jnp.logical_and(jnp.logical_not(valid), j == last))
    def _():
        o_ref[...] = jnp.zeros_like(o_ref)


def _ffn(x, wg, wu, wd, tile_expert, tile_valid, tm, tf, name, res=None):
    n, d = x.shape
    f = wg.shape[2]
    assert x.dtype == BF16 and f % tf == 0
    nj = f // tf

    def fj(i, j, tv):
        return jnp.where(tv[i] > 0, j, nj - 1)

    in_specs = [pl.BlockSpec((tm, d), lambda i, j, te, tv: (i, 0)),
                pl.BlockSpec((None, d, tf), lambda i, j, te, tv: (te[i], 0, fj(i, j, tv))),
                pl.BlockSpec((None, d, tf), lambda i, j, te, tv: (te[i], 0, fj(i, j, tv))),
                pl.BlockSpec((None, tf, d), lambda i, j, te, tv: (te[i], fj(i, j, tv), 0))]
    args = [x, wg, wu, wd]
    if res is not None:
        in_specs.append(pl.BlockSpec((tm, d), lambda i, j, te, tv: (i, 0)))
        args.append(res)
    return pl.pallas_call(
        functools.partial(_ffn_kernel, has_res=res is not None),
        grid_spec=pltpu.PrefetchScalarGridSpec(
            num_scalar_prefetch=2,
            grid=(n // tm, nj),
            in_specs=in_specs,
            out_specs=pl.BlockSpec((tm, d), lambda i, j, te, tv: (i, 0)),
            scratch_shapes=[pltpu.VMEM((tm, d), F32)]),
        out_shape=jax.ShapeDtypeStruct((n, d), F32),
        compiler_params=_params("parallel", "arbitrary"),
        name=name,
    )(tile_expert, tile_valid, *args)


def _router_kernel(h_ref, w_ref, b_ref, idx_ref, gate_ref, cnt_ref, run_ref, *, tm):
    i = pl.program_id(0)

    @pl.when(i == 0)
    def _():
        run_ref[...] = jnp.zeros_like(run_ref)

    logits = jnp.dot(h_ref[...], w_ref[...], preferred_element_type=F32) + b_ref[...]
    lane = lax.broadcasted_iota(jnp.int32, logits.shape, 1)
    m1 = jnp.max(logits, axis=1, keepdims=True)
    i1 = jnp.min(jnp.where(logits == m1, lane, LANES), axis=1, keepdims=True)
    oh1 = lane == i1
    rest = jnp.where(oh1, -jnp.inf, logits)
    m2 = jnp.max(rest, axis=1, keepdims=True)
    i2 = jnp.min(jnp.where(rest == m2, lane, LANES), axis=1, keepdims=True)
    oh2 = lane == i2
    e = jnp.exp(m2 - m1)
    g1 = 1.0 / (1.0 + e)
    g2 = e / (1.0 + e)

    row = lax.broadcasted_iota(jnp.int32, (tm, tm), 0)
    col = lax.broadcasted_iota(jnp.int32, (tm, tm), 1)
    tri = (col < row).astype(BF16)
    picked = jnp.logical_or(oh1, oh2).astype(BF16)
    before = jnp.dot(tri, picked, preferred_element_type=F32) + run_ref[...]
    r1 = jnp.sum(jnp.where(oh1, before, 0.0), axis=1, keepdims=True).astype(jnp.int32)
    r2 = jnp.sum(jnp.where(oh2, before, 0.0), axis=1, keepdims=True).astype(jnp.int32)
    run_ref[...] += jnp.sum(picked.astype(F32), axis=0, keepdims=True)

    idx_ref[...] = jnp.where(lane == 0, i1, jnp.where(lane == 1, i2,
                             jnp.where(lane == 2, r1, jnp.where(lane == 3, r2, 0))))
    gate_ref[...] = jnp.where(lane == 0, g1, jnp.where(lane == 1, g2, 0.0))
    cnt_ref[...] = run_ref[...]


def _router(h, w_r, b_r):
    n, d = h.shape
    ne = w_r.shape[1]
    tm = _tile(n, TOKEN_TILE)
    w_pad = jnp.zeros((d, LANES), BF16).at[:, :ne].set(w_r.astype(BF16))
    b_pad = jnp.full((1, LANES), NEG_INF, F32).at[0, :ne].set(b_r.astype(F32))
    return pl.pallas_call(
        functools.partial(_router_kernel, tm=tm),
        grid=(n // tm,),
        in_specs=[pl.BlockSpec((tm, d), lambda i: (i, 0)),
                  pl.BlockSpec((d, LANES), lambda i: (0, 0)),
                  pl.BlockSpec((1, LANES), lambda i: (0, 0))],
        out_specs=[pl.BlockSpec((tm, LANES), lambda i: (i, 0)),
                   pl.BlockSpec((tm, LANES), lambda i: (i, 0)),
                   pl.BlockSpec((1, LANES), lambda i: (0, 0))],
        out_shape=[jax.ShapeDtypeStruct((n, LANES), jnp.int32),
                   jax.ShapeDtypeStruct((n, LANES), F32),
                   jax.ShapeDtypeStruct((1, LANES), F32)],
        scratch_shapes=[pltpu.VMEM((1, LANES), F32)],
        compiler_params=_params("arbitrary"),
        name="moe_router",
    )(h, w_pad, b_pad)


def _row_copy(src, src_row, dst, dst_row, sem):
    return pltpu.make_async_copy(src.at[pl.ds(src_row, 1), :], dst.at[pl.ds(dst_row, 1), :], sem)


def _dispatch_kernel(src_ref, h_ref, o_ref, buf_ref, sem, *, tt):
    def issue(r, c):
        _row_copy(h_ref, src_ref[0, 0, r], buf_ref, r, sem).start()
        return c

    lax.fori_loop(0, tt, issue, 0, unroll=8)

    def drain(r, c):
        _row_copy(h_ref, 0, buf_ref, 0, sem).wait()
        return c

    lax.fori_loop(0, tt, drain, 0, unroll=8)
    o_ref[...] = buf_ref[...].astype(o_ref.dtype)


def _dispatch(h, src):
    d = h.shape[1]
    n_slots = src.shape[0]
    tt = _tile(n_slots, ROW_DMA_BYTES // (d * h.dtype.itemsize))
    return pl.pallas_call(
        functools.partial(_dispatch_kernel, tt=tt),
        grid=(n_slots // tt,),
        in_specs=[pl.BlockSpec((1, 1, tt), lambda i: (i, 0, 0), memory_space=pltpu.SMEM),
                  pl.BlockSpec(memory_space=pl.ANY)],
        out_specs=pl.BlockSpec((tt, d), lambda i: (i, 0)),
        out_shape=jax.ShapeDtypeStruct((n_slots, d), BF16),
        scratch_shapes=[pltpu.VMEM((tt, d), h.dtype), pltpu.SemaphoreType.DMA(())],
        compiler_params=_params("arbitrary"),
        name="moe_dispatch",
    )(src.reshape(n_slots // tt, 1, tt), h)


def _combine_kernel(pos_ref, x_ref, gate_ref, y_ref, g_ref, *rest, tt, want_x):
    if want_x:
        x_out_ref, o_ref, buf_ref, sem = rest
    else:
        o_ref, buf_ref, sem = rest

    def issue(r, c):
        for k in range(TOP_K):
            _row_copy(y_ref, pos_ref[0, 0, TOP_K * r + k], buf_ref.at[k], r, sem).start()
        return c

    lax.fori_loop(0, tt, issue, 0, unroll=8)

    def drain(r, c):
        for k in range(TOP_K):
            _row_copy(y_ref, 0, buf_ref.at[k], 0, sem).wait()
        return c

    lax.fori_loop(0, tt, drain, 0, unroll=8)
    g = gate_ref[...]
    x = x_ref[...] + (g[:, 0:1] * buf_ref[0] + g[:, 1:2] * buf_ref[1])
    if want_x:
        x_out_ref[...] = x
    o_ref[...] = _rms(x, g_ref[...]).astype(o_ref.dtype)


def _combine_norm(x, gates, y, pos, g, row0, n_rows, norm_dtype, want_x):
    n, d = x.shape
    tt = _tile(n_rows, ROW_DMA_BYTES // (TOP_K * d * y.dtype.itemsize))
    assert n % tt == 0 and row0 % tt == 0
    off = row0 // tt
    pos3 = pos.reshape(n // tt, 1, TOP_K * tt)
    out = pl.BlockSpec((tt, d), lambda i: (i, 0))
    return pl.pallas_call(
        functools.partial(_combine_kernel, tt=tt, want_x=want_x),
        grid=(n_rows // tt,),
        in_specs=[pl.BlockSpec((1, 1, TOP_K * tt), lambda i: (off + i, 0, 0), memory_space=pltpu.SMEM),
                  pl.BlockSpec((tt, d), lambda i: (off + i, 0)),
                  pl.BlockSpec((tt, LANES), lambda i: (off + i, 0)),
                  pl.BlockSpec(memory_space=pl.ANY),
                  pl.BlockSpec((1, d), lambda i: (0, 0))],
        out_specs=[out] * (2 if want_x else 1),
        out_shape=([jax.ShapeDtypeStruct((n_rows, d), F32)] if want_x else [])
                  + [jax.ShapeDtypeStruct((n_rows, d), norm_dtype)],
        scratch_shapes=[pltpu.VMEM((TOP_K, tt, d), F32), pltpu.SemaphoreType.DMA(())],
        compiler_params=_params("arbitrary"),
        name="moe_combine",
    )(pos3, x, gates, y, g.reshape(1, d).astype(F32))


def _moe_route(h_bf, w_r, b_r, ne, tm):
    n = h_bf.shape[0]
    idx, gates, counts = _router(h_bf, w_r, b_r)
    counts = counts[0, :ne].astype(jnp.int32)
    padded = ((counts + tm - 1) // tm) * tm
    ends = jnp.cumsum(padded)
    offsets = ends - padded
    expert = idx[:, :TOP_K]
    pos = jnp.zeros_like(expert)
    for e in range(ne):
        pos = jnp.where(expert == e, offsets[e], pos)
    pos = pos + idx[:, TOP_K:2 * TOP_K]
    n_tiles = -(-(n * TOP_K) // tm) + ne
    tile_start = jnp.arange(n_tiles, dtype=jnp.int32) * tm
    tile_valid = (tile_start < ends[-1]).astype(jnp.int32)
    tile_expert = jnp.minimum(jnp.sum((tile_start[:, None] >= ends[None, :]).astype(jnp.int32), axis=1),
                              ne - 1)
    last_expert = jnp.max(jnp.where(counts > 0, jnp.arange(ne, dtype=jnp.int32), 0))
    tile_expert = jnp.where(tile_valid > 0, tile_expert, last_expert)
    token = jnp.broadcast_to(jnp.arange(n, dtype=jnp.int32)[:, None], pos.shape)
    spread = jnp.arange(n_tiles * tm, dtype=jnp.int32) % n
    src = spread.at[pos.reshape(-1)].set(token.reshape(-1), unique_indices=True)
    return pos, src, gates, tile_expert, tile_valid


def kernel(x_prompt, x_sample, cache_diff_k, cache_diff_v, cache_sb_k, cache_sb_v, rel_bias, norm_mix_g, w_in, lambda_q1, lambda_k1, lambda_q2, lambda_k2, subln_g, w_branch_a, w_branch_b, w_out, norm_ffn_g, dense_w_gate, dense_w_up, dense_w_down, router_w, router_b, moe_w_gate, moe_w_up, moe_w_down, final_norm_g):
    batch, seq, d = x_prompt.shape
    db, ts, _ = x_sample.shape
    depth, _, past, heads, w2 = cache_diff_k.shape
    d_a = w2 // 2
    d_b = cache_sb_k.shape[-1]
    wid = heads * w2
    n_p = batch * seq
    n_s = db * ts
    n_tok = n_p + n_s
    assert d % wid == 0 and w_in.shape[2] == 6 * wid + 2 * d
    assert cache_sb_k.shape[3] * d_b == wid and cache_diff_v.shape[-1] == w2

    x, h = _join_norm(x_prompt.reshape(n_p, d), x_sample.reshape(n_s, d), norm_mix_g[0])

    t_attn = _tile(seq, ATTN_TILE)
    pos_q = past + jnp.arange(ts, dtype=jnp.int32)
    bias_tiles = _prompt_bias_tiles(rel_bias, t_attn)
    bias_c = _masked_bias(rel_bias, pos_q, jnp.arange(past, dtype=jnp.int32))
    bias_n = _masked_bias(rel_bias, pos_q, pos_q)

    caches = [c.reshape(depth, db, past * heads, w2)
              for c in (cache_diff_k, cache_diff_v, cache_sb_k, cache_sb_v)]
    cache_p = [None] * 4
    cache_s = [None] * 4
    kv_cols = (1, 2, 4, 5)
    for l in range(depth):
        lam_init = 0.8 - 0.6 * math.exp(-0.3 * l)
        (q,) = _proj(h, w_in, l, lambda j: 3 * j, 2, wid,
                     [((n_tok, 2 * wid), BF16, (_tile(n_tok, TOKEN_TILE), wid), lambda j, i: (i, j))],
                     "proj_q")
        (gates,) = _proj(h, w_in, l, lambda j: j + 6, 2 * d // wid, wid,
                         [((n_tok, 2 * d), F32, (_tile(n_tok, TOKEN_TILE), wid), lambda j, i: (i, j))],
                         "proj_gate")
        for c in range(4):
            cache_p[c] = _proj_cache_rows(h, w_in, l, kv_cols[c], wid, 0, n_p, cache_p[c], "proj_kv_prompt")
            cache_s[c] = _proj_cache_rows(h, w_in, l, kv_cols[c], wid, n_p, n_s, cache_s[c], "proj_kv_sample")

        lam_vecs = [v[l].reshape(1, d_a).astype(F32) for v in (lambda_q1, lambda_k1, lambda_q2, lambda_k2)]
        g_sub = subln_g[l].reshape(1, w2).astype(F32)
        oa, ob = _prompt_attention(q, cache_p, l, bias_tiles, lam_vecs, g_sub, n_tok, batch, seq, heads, d_a,
                                   d_b, t_attn, lam_init)
        oa, ob = _sample_attention(q, cache_s, caches, l, bias_c, bias_n, lam_vecs, g_sub, oa, ob, n_p, db,
                                   ts, heads, d_a, d_b, lam_init)
        merged = _merge(oa, ob, gates, w_branch_a[l].astype(BF16), w_branch_b[l].astype(BF16))

        j = l // 2
        last = l + 1 == depth
        tm = _tile(n_tok, TOKEN_TILE)
        if l % 2 == 0:
            x, h2 = _out_proj(merged, w_out[l].astype(BF16), x, norm_ffn_g[l], [BF16])
            n_tiles = n_tok // tm
            x = _ffn(h2, dense_w_gate[j:j + 1].astype(BF16), dense_w_up[j:j + 1].astype(BF16),
                     dense_w_down[j:j + 1].astype(BF16), jnp.zeros((n_tiles,), jnp.int32),
                     jnp.ones((n_tiles,), jnp.int32), tm, _tile(dense_w_gate.shape[2], FF_TILE),
                     "dense_ffn", res=x)
            if last:
                (y_prompt,) = _rmsnorm(x, final_norm_g, [F32], "norm_final", 0, n_p)
                (y_sample,) = _rmsnorm(x, final_norm_g, [F32], "norm_final", n_p, n_s)
            else:
                (h,) = _rmsnorm(x, norm_mix_g[l + 1], [BF16], "norm_mix")
        else:
            x, h2, h2_f32 = _out_proj(merged, w_out[l].astype(BF16), x, norm_ffn_g[l], [BF16, F32])
            n_moe, ne, _, f_e = moe_w_gate.shape
            tm_e = MOE_ROW_TILE
            pos, src, route_g, tile_expert, tile_valid = _moe_route(h2, router_w[j], router_b[j], ne, tm_e)
            xs = _dispatch(h2_f32, src)
            y = _ffn(xs, moe_w_gate.reshape(n_moe * ne, d, f_e), moe_w_up.reshape(n_moe * ne, d, f_e),
                     moe_w_down.reshape(n_moe * ne, f_e, d), tile_expert + j * ne, tile_valid, tm_e,
                     _tile(f_e, MOE_FF_TILE), "moe_ffn")
            if last:
                (y_prompt,) = _combine_norm(x, route_g, y, pos, final_norm_g, 0, n_p, F32, False)
                (y_sample,) = _combine_norm(x, route_g, y, pos, final_norm_g, n_p, n_s, F32, False)
            else:
                x, h = _combine_norm(x, route_g, y, pos, norm_mix_g[l + 1], 0, n_tok, BF16, True)

    outs_p = [c.reshape(depth, batch, seq, heads, w2) for c in cache_p]
    outs_s = [c.reshape(depth, db, ts, heads, w2) for c in cache_s]
    return (y_prompt.reshape(batch, seq, d), y_sample.reshape(db, ts, d), *outs_p, *outs_s)
```
